```python
import jax, jax.numpy as jnp
from jax import lax
import numpy as np

D_MODEL = 1024
BATCH = 16
SEQ = 2048
DEPTH = 2

N_MIXERS = 2
MEM_LEN = 256
EPS = 1e-6
HG_HEADS = 8
HG_DIM = D_MODEL // HG_HEADS
HG_WIDTH = HG_HEADS * HG_DIM
HG_CHUNK = 64
GM_CHUNK = 128
GM_GROUPS = 8
GM_WIDTH = 2 * D_MODEL
GM_GROUP_DIM = GM_WIDTH // GM_GROUPS
XA_HEADS = 4
XA_DIM = D_MODEL // XA_HEADS
XA_WIDTH = XA_HEADS * XA_DIM
D_FF = 2816
N_HGRN = (DEPTH + 1) // 2
N_GMLP = DEPTH // 2
HG_IN = 4 * HG_WIDTH + XA_WIDTH
GM_IN = 2 * GM_WIDTH + XA_WIDTH

kernel_name = "hybrid_hgrn2_gmlp_memory_macaron"


def rmsnorm(x, g):
    xf = x.astype(jnp.float32)
    y = xf * lax.rsqrt(jnp.mean(xf * xf, axis=-1, keepdims=True) + EPS)
    return (y * g.astype(jnp.float32)).astype(x.dtype)


def layernorm(x, g, b):
    xf = x.astype(jnp.float32)
    mu = jnp.mean(xf, axis=-1, keepdims=True)
    xc = xf - mu
    y = xc * lax.rsqrt(jnp.mean(xc * xc, axis=-1, keepdims=True) + EPS)
    return (y * g.astype(jnp.float32) + b.astype(jnp.float32)).astype(x.dtype)


def swiglu_ffn(h, w_in, w_out):
    gate, up = jnp.split(h @ w_in, 2, axis=-1)
    return (jax.nn.silu(gate) * up) @ w_out


def memory_attention(zq, mem_k, mem_v):
    B, T, _ = zq.shape
    q = zq.reshape(B, T, XA_HEADS, XA_DIM)
    s = jnp.einsum('bthd,bmhd->bhtm', q, mem_k).astype(jnp.float32) * (XA_DIM ** -0.5)
    p = jax.nn.softmax(s, axis=-1).astype(mem_v.dtype)
    o = jnp.einsum('bhtm,bmhd->bthd', p, mem_v)
    return o.reshape(B, T, XA_WIDTH)


def hgrn2_recurrence(q, k, v, log_f):
    B, T, H, Dk = q.shape
    Dv = v.shape[-1]
    n = T // HG_CHUNK

    def to_chunks(a):
        return a.reshape(B, n, HG_CHUNK, H, a.shape[-1]).transpose(1, 0, 3, 2, 4)

    qc, kc, vc, lc = to_chunks(q), to_chunks(k), to_chunks(v), to_chunks(log_f)
    causal = jnp.tril(jnp.ones((HG_CHUNK, HG_CHUNK), dtype=bool))

    def step(S, inp):
        qn, kn, vn, ln = inp
        b = jnp.cumsum(ln, axis=2)
        b_last = b[:, :, -1:, :]
        q_dec = qn * jnp.exp(b)
        k_inv = kn * jnp.exp(-b)
        a = jnp.einsum('bhtk,bhsk->bhts', q_dec, k_inv)
        a = jnp.where(causal, a, 0.0)
        o = jnp.einsum('bhts,bhsv->bhtv', a, vn) + jnp.einsum('bhtk,bhkv->bhtv', q_dec, S)
        S_new = jnp.exp(b_last[:, :, 0, :])[..., None] * S + jnp.einsum(
            'bhsk,bhsv->bhkv', kn * jnp.exp(b_last - b), vn)
        return S_new, o

    S0 = jnp.zeros((B, H, Dk, Dv), jnp.float32)
    _, o = lax.scan(step, S0, (qc, kc, vc, lc))
    return o.transpose(1, 0, 3, 2, 4).reshape(B, T, H, Dv)


def hgrn2_mixer(zq, zf, zi, zg, lb, gnorm_g):
    B, T, _ = zq.shape
    shp = (B, T, HG_HEADS, HG_DIM)
    q = jax.nn.silu(zq.astype(jnp.float32)).reshape(shp)
    lbh = lb.astype(jnp.float32).reshape(HG_HEADS, HG_DIM)
    f = lbh + (1.0 - lbh) * jax.nn.sigmoid(zf.astype(jnp.float32).reshape(shp))
    k = 1.0 - f
    v = zi.astype(jnp.float32).reshape(shp)
    o = hgrn2_recurrence(q, k, v, jnp.log(f))
    o = rmsnorm(o, gnorm_g) * jax.nn.silu(zg.astype(jnp.float32).reshape(shp))
    return o.reshape(B, T, HG_WIDTH).astype(zq.dtype)


def chunked_spatial_gating(zu, zv, ln_g, ln_b, w_s, b_s):
    B, T, _ = zu.shape
    n = T // GM_CHUNK
    u = jax.nn.gelu(zu, approximate=False)
    v = layernorm(jax.nn.gelu(zv, approximate=False), ln_g, ln_b)
    vc = v.reshape(B, n, GM_CHUNK, GM_GROUPS, GM_GROUP_DIM)
    causal = jnp.tril(jnp.ones((GM_CHUNK, GM_CHUNK), dtype=bool))
    w = jnp.where(causal[None], w_s, 0.0).astype(v.dtype)
    mixed = jnp.einsum('gts,bnsgc->bntgc', w, vc) + b_s.T.astype(v.dtype)[None, None, :, :, None]
    return u * mixed.reshape(B, T, GM_WIDTH)


def setup_inputs(seed: int = 0) -> dict:
    key = jax.random.key(seed)
    ks = iter(jax.random.split(key, 32))

    def nrm(shape, scale):
        return jax.random.normal(next(ks), shape, jnp.float32) * scale

    def gain(shape):
        return 1.0 + nrm(shape, 0.05)

    return {
        "x": nrm((BATCH, SEQ, D_MODEL), 1.0),
        "mem": nrm((BATCH, MEM_LEN, D_MODEL), 1.0),
        "mem_norm": gain((D_MODEL,)),
        "lb_logits": nrm((DEPTH + 1, HG_WIDTH), 0.1),
        "ffn1_norm": gain((DEPTH, D_MODEL)),
        "ffn1_w_in": nrm((DEPTH, D_MODEL, 2 * D_FF), D_MODEL ** -0.5),
        "ffn1_w_out": nrm((DEPTH, D_FF, D_MODEL), D_FF ** -0.5),
        "mix_norm": gain((DEPTH, D_MODEL)),
        "mem_w_kv": nrm((DEPTH, D_MODEL, 2 * XA_WIDTH), D_MODEL ** -0.5),
        "hgrn_w_in": nrm((N_HGRN, D_MODEL, HG_IN), D_MODEL ** -0.5),
        "hgrn_gnorm": gain((N_HGRN, HG_DIM)),
        "hgrn_w_out": nrm((N_HGRN, HG_WIDTH + XA_WIDTH, D_MODEL), (HG_WIDTH + XA_WIDTH) ** -0.5),
        "gmlp_w_in": nrm((N_GMLP, D_MODEL, GM_IN), D_MODEL ** -0.5),
        "gmlp_ln_g": gain((N_GMLP, GM_WIDTH)),
        "gmlp_ln_b": nrm((N_GMLP, GM_WIDTH), 0.02),
        "gmlp_w_s": nrm((N_GMLP, GM_GROUPS, GM_CHUNK, GM_CHUNK), GM_CHUNK ** -0.5),
        "gmlp_b_s": 1.0 + nrm((N_GMLP, GM_GROUPS, GM_CHUNK), 0.1),
        "gmlp_w_out": nrm((N_GMLP, GM_WIDTH + XA_WIDTH, D_MODEL), (GM_WIDTH + XA_WIDTH) ** -0.5),
        "ffn2_norm": gain((DEPTH, D_MODEL)),
        "ffn2_w_in": nrm((DEPTH, D_MODEL, 2 * D_FF), D_MODEL ** -0.5),
        "ffn2_w_out": nrm((DEPTH, D_FF, D_MODEL), D_FF ** -0.5),
        "final_norm": gain((D_MODEL,)),
    }


def reference(x, mem, mem_norm, lb_logits, ffn1_norm, ffn1_w_in, ffn1_w_out, mix_norm, mem_w_kv,
              hgrn_w_in, hgrn_gnorm, hgrn_w_out, gmlp_w_in, gmlp_ln_g, gmlp_ln_b, gmlp_w_s, gmlp_b_s,
              gmlp_w_out, ffn2_norm, ffn2_w_in, ffn2_w_out, final_norm):
    B, T, _ = x.shape
    M = mem.shape[1]
    mem_n = rmsnorm(mem, mem_norm)
    lower_bounds = jnp.cumsum(jax.nn.softmax(lb_logits.astype(jnp.float32), axis=0), axis=0)

    for i in range(DEPTH):
        x = x + 0.5 * swiglu_ffn(rmsnorm(x, ffn1_norm[i]), ffn1_w_in[i], ffn1_w_out[i])

        h = rmsnorm(x, mix_norm[i])
        mk, mv = jnp.split(mem_n @ mem_w_kv[i], 2, axis=-1)
        mk = mk.reshape(B, M, XA_HEADS, XA_DIM)
        mv = mv.reshape(B, M, XA_HEADS, XA_DIM)
        j = i // N_MIXERS
        if i % N_MIXERS == 0:
            z = h @ hgrn_w_in[j]
            zq, zf, zi, zg, zx = jnp.split(z, [HG_WIDTH, 2 * HG_WIDTH, 3 * HG_WIDTH, 4 * HG_WIDTH], axis=-1)
            o_mix = hgrn2_mixer(zq, zf, zi, zg, lower_bounds[i], hgrn_gnorm[j])
            w_out = hgrn_w_out[j]
        else:
            z = h @ gmlp_w_in[j]
            zu, zv, zx = jnp.split(z, [GM_WIDTH, 2 * GM_WIDTH], axis=-1)
            o_mix = chunked_spatial_gating(zu, zv, gmlp_ln_g[j], gmlp_ln_b[j], gmlp_w_s[j], gmlp_b_s[j])
            w_out = gmlp_w_out[j]
        o_mem = memory_attention(zx, mk, mv)
        x = x + jnp.concatenate([o_mix, o_mem], axis=-1) @ w_out

        x = x + 0.5 * swiglu_ffn(rmsnorm(x, ffn2_norm[i]), ffn2_w_in[i], ffn2_w_out[i])

    return rmsnorm(x, final_norm)
```

```python
import functools

import jax
import jax.numpy as jnp
from jax import lax
from jax.experimental import pallas as pl
from jax.experimental.pallas import tpu as pltpu

EPS = 1e-6
HG_HEADS = 8
HG_CHUNK = 64
GM_CHUNK = 128
GM_GROUPS = 8
XA_HEADS = 4

V7X_VMEM_BYTES = 64 * 1024 * 1024
VMEM_LIMIT_BYTES = V7X_VMEM_BYTES - 8 * 1024 * 1024
MXU_COLS = 256
FFN_ROWS = 512
MIX_ROWS = 256

F32 = jnp.float32
BF16 = jnp.bfloat16


def _dot(a, b):
    return jnp.dot(a, b, preferred_element_type=F32)


def _dot_nt(a, b):
    return lax.dot_general(a, b, (((1,), (1,)), ((), ())), preferred_element_type=F32)


def _dot_tn(a, b):
    return lax.dot_general(a, b, (((0,), (0,)), ((), ())), preferred_element_type=F32)


def _rms(x, g):
    return x * lax.rsqrt(jnp.mean(x * x, axis=-1, keepdims=True) + EPS) * g


def _sigmoid(x):
    return 1.0 / (1.0 + jnp.exp(-x))


def _const_spec(shape):
    nd = len(shape)
    return pl.BlockSpec(shape, lambda *_: (0,) * nd, pipeline_mode=pl.Buffered(1))


def _params(n_grid):
    return pltpu.CompilerParams(
        dimension_semantics=("arbitrary",) * n_grid,
        vmem_limit_bytes=VMEM_LIMIT_BYTES,
    )


def _mem_kv_kernel(mem_ref, g_ref, wkv_ref, kt_ref, v_ref, *, scale):
    width = v_ref.shape[-1]
    mn = _rms(mem_ref[0], g_ref[...]).astype(BF16)
    kv = _dot(mn, wkv_ref[0])
    kt_ref[0, 0] = (kv[:, :width] * scale).T.astype(BF16)
    v_ref[0, 0] = kv[:, width:].astype(BF16)


def _mem_kv(mem, mem_norm, w_kv_bf16):
    B, M, D = mem.shape
    depth, _, two_w = w_kv_bf16.shape
    width = two_w // 2
    scale = float((width // XA_HEADS) ** -0.5)
    return pl.pallas_call(
        functools.partial(_mem_kv_kernel, scale=scale),
        out_shape=(jax.ShapeDtypeStruct((depth, B, width, M), BF16),
                   jax.ShapeDtypeStruct((depth, B, M, width), BF16)),
        grid=(depth, B),
        in_specs=[pl.BlockSpec((1, M, D), lambda l, b: (b, 0, 0)),
                  pl.BlockSpec((1, D), lambda l, b: (0, 0)),
                  pl.BlockSpec((1, D, two_w), lambda l, b: (l, 0, 0))],
        out_specs=(pl.BlockSpec((1, 1, width, M), lambda l, b: (l, b, 0, 0)),
                   pl.BlockSpec((1, 1, M, width), lambda l, b: (l, b, 0, 0))),
        compiler_params=_params(2),
        name="mem_kv",
    )(mem, mem_norm.reshape(1, D), w_kv_bf16)


def _ffn_kernel(x_ref, g_ref, win_ref, wout_ref, fg_ref, o_ref, xn_ref, act_ref, *, final_norm):
    d_ff = wout_ref.shape[0]
    x = x_ref[...]
    xn_ref[...] = _rms(x, g_ref[...]).astype(BF16)
    for c in range(d_ff // MXU_COLS):
        lo = c * MXU_COLS
        xn = xn_ref[...]
        gate = _dot(xn, win_ref[:, lo:lo + MXU_COLS])
        up = _dot(xn, win_ref[:, d_ff + lo:d_ff + lo + MXU_COLS])
        act_ref[:, lo:lo + MXU_COLS] = (gate * _sigmoid(gate) * up).astype(BF16)
    y = x_ref[...] + 0.5 * _dot(act_ref[...], wout_ref[...])
    if final_norm:
        y = _rms(y, fg_ref[...])
    o_ref[...] = y


def _ffn(x2d, norm_g, w_in_bf16, w_out_bf16, final_g=None):
    N, D = x2d.shape
    d_ff = w_out_bf16.shape[0]
    assert d_ff % MXU_COLS == 0 and N % FFN_ROWS == 0
    final_norm = final_g is not None
    fg = final_g if final_norm else norm_g
    return pl.pallas_call(
        functools.partial(_ffn_kernel, final_norm=final_norm),
        out_shape=jax.ShapeDtypeStruct((N, D), F32),
        grid=(N // FFN_ROWS,),
        in_specs=[pl.BlockSpec((FFN_ROWS, D), lambda i: (i, 0)),
                  _const_spec((1, D)),
                  _const_spec((D, 2 * d_ff)),
                  _const_spec((d_ff, D)),
                  _const_spec((1, D))],
        out_specs=pl.BlockSpec((FFN_ROWS, D), lambda i: (i, 0)),
        scratch_shapes=[pltpu.VMEM((FFN_ROWS, D), BF16),
                        pltpu.VMEM((FFN_ROWS, d_ff), BF16)],
        compiler_params=_params(1),
        name="ffn_final" if final_norm else "ffn",
    )(x2d, norm_g.reshape(1, D), w_in_bf16, w_out_bf16, fg.reshape(1, D))


def _mem_attention(qx_ref, kt_ref, mv_ref, ocat_ref, col0):
    xa_dim = qx_ref.shape[1] // XA_HEADS
    for h in range(XA_HEADS):
        cols = slice(h * xa_dim, (h + 1) * xa_dim)
        s = _dot(qx_ref[:, cols], kt_ref[0, 0, cols, :])
        p = jnp.exp(s - jnp.max(s, axis=-1, keepdims=True))
        denom = jnp.sum(p, axis=-1, keepdims=True)
        o = _dot(p.astype(BF16), mv_ref[0, 0, :, cols]) / denom
        ocat_ref[:, col0 + h * xa_dim:col0 + (h + 1) * xa_dim] = o.astype(BF16)


def _hgrn_kernel(x_ref, g_ref, lbl_ref, win_ref, gn_ref, kt_ref, mv_ref, wout_ref, o_ref,
                 h_ref, q_ref, k_ref, lf_ref, v_ref, sg_ref, qx_ref, b_ref, st_ref, ocat_ref,
                 *, layer_idx):
    rows, D = h_ref.shape
    hd = D // HG_HEADS
    C = HG_CHUNK

    @pl.when(pl.program_id(1) == 0)
    def _():
        st_ref[...] = jnp.zeros_like(st_ref)

    h_ref[...] = _rms(x_ref[0], g_ref[...]).astype(BF16)

    n_lb = lbl_ref.shape[0]
    lrows = [lbl_ref[r:r + 1, :] for r in range(n_lb)]
    lmax = functools.reduce(jnp.maximum, lrows)
    lexp = [jnp.exp(r - lmax) for r in lrows]
    lb = functools.reduce(lambda a, b: a + b, lexp[:layer_idx + 1]) / functools.reduce(lambda a, b: a + b, lexp)

    hb = h_ref[...]
    zq = _dot(hb, win_ref[:, 0:D])
    q_ref[...] = zq * _sigmoid(zq)
    f = lb + (1.0 - lb) * _sigmoid(_dot(hb, win_ref[:, D:2 * D]))
    k_ref[...] = 1.0 - f
    lf_ref[...] = jnp.log(f)
    v_ref[...] = _dot(hb, win_ref[:, 2 * D:3 * D]).astype(BF16)
    zg = _dot(hb, win_ref[:, 3 * D:4 * D])
    sg_ref[...] = zg * _sigmoid(zg)
    qx_ref[...] = _dot(hb, win_ref[:, 4 * D:]).astype(BF16)

    ri = lax.broadcasted_iota(jnp.int32, (C, C), 0)
    ci = lax.broadcasted_iota(jnp.int32, (C, C), 1)
    causal = ci <= ri
    tril = jnp.where(causal, 1.0, 0.0).astype(BF16)
    gn = gn_ref[...]

    def chunk(c, carry):
        r0 = pl.multiple_of(c * C, C)
        rs = pl.ds(r0, C)
        lf = lf_ref[rs, :]
        lf_hi = lf.astype(BF16)
        rem = lf - lf_hi.astype(F32)
        lf_mid = rem.astype(BF16)
        lf_lo = (rem - lf_mid.astype(F32)).astype(BF16)
        b_ref[...] = _dot(tril, lf_hi) + _dot(tril, lf_mid) + _dot(tril, lf_lo)
        for hh in range(HG_HEADS):
            cols = slice(hh * hd, (hh + 1) * hd)
            b = b_ref[:, cols]
            b_last = b_ref[C - 1:C, cols]
            kk = k_ref[rs, cols]
            q_dec = (q_ref[rs, cols] * jnp.exp(b)).astype(BF16)
            k_inv = (kk * jnp.exp(-b)).astype(BF16)
            k_rem = (kk * jnp.exp(b_last - b)).astype(BF16)
            vv = v_ref[rs, cols]
            a = jnp.where(causal, _dot_nt(q_dec, k_inv), 0.0).astype(BF16)
            st = st_ref[hh]
            o = _dot(a, vv) + _dot_nt(q_dec, st.astype(BF16))
            st_ref[hh] = st * jnp.exp(b_last) + _dot_tn(vv, k_rem)
            o = o * lax.rsqrt(jnp.mean(o * o, axis=-1, keepdims=True) + EPS) * gn
            ocat_ref[rs, cols] = (o * sg_ref[rs, cols]).astype(BF16)
        return carry

    lax.fori_loop(0, rows // C, chunk, 0)

    _mem_attention(qx_ref, kt_ref, mv_ref, ocat_ref, D)
    o_ref[0] = x_ref[0] + _dot(ocat_ref[...], wout_ref[...])


def _hgrn_mix(x, norm_g, lb_logits, layer_idx, w_in_bf16, gnorm, kt, mv, w_out_bf16):
    B, T, D = x.shape
    rows = MIX_ROWS
    hd = D // HG_HEADS
    M = kt.shape[-1]
    xw = kt.shape[-2]
    n_in = w_in_bf16.shape[1]
    n_cat = w_out_bf16.shape[0]
    assert T % rows == 0 and rows % HG_CHUNK == 0
    return pl.pallas_call(
        functools.partial(_hgrn_kernel, layer_idx=layer_idx),
        out_shape=jax.ShapeDtypeStruct((B, T, D), F32),
        grid=(B, T // rows),
        in_specs=[pl.BlockSpec((1, rows, D), lambda b, t: (b, t, 0)),
                  _const_spec((1, D)),
                  _const_spec(lb_logits.shape),
                  _const_spec((D, n_in)),
                  _const_spec((1, hd)),
                  pl.BlockSpec((1, 1, xw, M), lambda b, t: (layer_idx, b, 0, 0)),
                  pl.BlockSpec((1, 1, M, xw), lambda b, t: (layer_idx, b, 0, 0)),
                  _const_spec((n_cat, D))],
        out_specs=pl.BlockSpec((1, rows, D), lambda b, t: (b, t, 0)),
        scratch_shapes=[pltpu.VMEM((rows, D), BF16),
                        pltpu.VMEM((rows, D), F32),
                        pltpu.VMEM((rows, D), F32),
                        pltpu.VMEM((rows, D), F32),
                        pltpu.VMEM((rows, D), BF16),
                        pltpu.VMEM((rows, D), F32),
                        pltpu.VMEM((rows, xw), BF16),
                        pltpu.VMEM((HG_CHUNK, D), F32),
                        pltpu.VMEM((HG_HEADS, hd, hd), F32),
                        pltpu.VMEM((rows, n_cat), BF16)],
        compiler_params=_params(2),
        name="hgrn_mix",
    )(x, norm_g.reshape(1, D), lb_logits, w_in_bf16, gnorm.reshape(1, hd), kt, mv, w_out_bf16)


def _gmlp_kernel(x_ref, g_ref, win_ref, lng_ref, lnb_ref, ws_ref, bs_ref, kt_ref, mv_ref, wout_ref, o_ref,
                 h_ref, u_ref, v_ref, qx_ref, wm_ref, ocat_ref):
    rows, D = h_ref.shape
    W = u_ref.shape[1]
    C = GM_CHUNK
    gd = W // GM_GROUPS

    h_ref[...] = _rms(x_ref[...], g_ref[...]).astype(BF16)
    hb = h_ref[...]
    inv_sqrt2 = 0.7071067811865476
    zu = _dot(hb, win_ref[:, 0:W])
    u_ref[...] = 0.5 * zu * (1.0 + lax.erf(zu * inv_sqrt2))
    zv = _dot(hb, win_ref[:, W:2 * W])
    gv = 0.5 * zv * (1.0 + lax.erf(zv * inv_sqrt2))
    mu = jnp.mean(gv, axis=-1, keepdims=True)
    xc = gv - mu
    vn = xc * lax.rsqrt(jnp.mean(xc * xc, axis=-1, keepdims=True) + EPS)
    v_ref[...] = (vn * lng_ref[...] + lnb_ref[...]).astype(BF16)
    qx_ref[...] = _dot(hb, win_ref[:, 2 * W:]).astype(BF16)

    ri = lax.broadcasted_iota(jnp.int32, (C, C), 0)
    ci = lax.broadcasted_iota(jnp.int32, (C, C), 1)
    causal = ci <= ri
    for g in range(GM_GROUPS):
        wm_ref[g] = jnp.where(causal, ws_ref[g], 0.0).astype(BF16)

    for c in range(rows // C):
        rs = slice(c * C, (c + 1) * C)
        for g in range(GM_GROUPS):
            cols = slice(g * gd, (g + 1) * gd)
            mixed = _dot(wm_ref[g], v_ref[rs, cols]) + bs_ref[:, g:g + 1]
            ocat_ref[rs, cols] = (u_ref[rs, cols] * mixed).astype(BF16)

    _mem_attention(qx_ref, kt_ref, mv_ref, ocat_ref, W)
    o_ref[...] = x_ref[...] + _dot(ocat_ref[...], wout_ref[...])


def _gmlp_mix(x2d, seq_len, norm_g, layer_idx, w_in_bf16, ln_g, ln_b, w_s, b_s, kt, mv, w_out_bf16):
    N, D = x2d.shape
    rows = MIX_ROWS
    W = ln_g.shape[0]
    M = kt.shape[-1]
    xw = kt.shape[-2]
    n_in = w_in_bf16.shape[1]
    n_cat = w_out_bf16.shape[0]
    assert seq_len % rows == 0 and rows % GM_CHUNK == 0
    per_seq = seq_len // rows
    return pl.pallas_call(
        _gmlp_kernel,
        out_shape=jax.ShapeDtypeStruct((N, D), F32),
        grid=(N // rows,),
        in_specs=[pl.BlockSpec((rows, D), lambda i: (i, 0)),
                  _const_spec((1, D)),
                  _const_spec((D, n_in)),
                  _const_spec((1, W)),
                  _const_spec((1, W)),
                  _const_spec(w_s.shape),
                  _const_spec((GM_CHUNK, GM_GROUPS)),
                  pl.BlockSpec((1, 1, xw, M), lambda i: (layer_idx, i // per_seq, 0, 0)),
                  pl.BlockSpec((1, 1, M, xw), lambda i: (layer_idx, i // per_seq, 0, 0)),
                  _const_spec((n_cat, D))],
        out_specs=pl.BlockSpec((rows, D), lambda i: (i, 0)),
        scratch_shapes=[pltpu.VMEM((rows, D), BF16),
                        pltpu.VMEM((rows, W), F32),
                        pltpu.VMEM((rows, W), BF16),
                        pltpu.VMEM((rows, xw), BF16),
                        pltpu.VMEM(w_s.shape, BF16),
                        pltpu.VMEM((rows, n_cat), BF16)],
        compiler_params=_params(1),
        name="gmlp_mix",
    )(x2d, norm_g.reshape(1, D), w_in_bf16, ln_g.reshape(1, W), ln_b.reshape(1, W), w_s, b_s.T, kt, mv,
      w_out_bf16)


def kernel(x, mem, mem_norm, lb_logits, ffn1_norm, ffn1_w_in, ffn1_w_out, mix_norm, mem_w_kv,
           hgrn_w_in, hgrn_gnorm, hgrn_w_out, gmlp_w_in, gmlp_ln_g, gmlp_ln_b, gmlp_w_s, gmlp_b_s,
           gmlp_w_out, ffn2_norm, ffn2_w_in, ffn2_w_out, final_norm):
    B, T, D = x.shape
    depth = ffn1_norm.shape[0]
    bf = lambda w: w.astype(BF16)

    kt, mv = _mem_kv(mem, mem_norm, bf(mem_w_kv))
    for i in range(depth):
        x = _ffn(x.reshape(B * T, D), ffn1_norm[i], bf(ffn1_w_in[i]), bf(ffn1_w_out[i])).reshape(B, T, D)
        j = i // 2
        if i % 2 == 0:
            x = _hgrn_mix(x, mix_norm[i], lb_logits, i, bf(hgrn_w_in[j]), hgrn_gnorm[j], kt, mv,
                          bf(hgrn_w_out[j]))
        else:
            x = _gmlp_mix(x.reshape(B * T, D), T, mix_norm[i], i, bf(gmlp_w_in[j]), gmlp_ln_g[j],
                          gmlp_ln_b[j], gmlp_w_s[j], gmlp_b_s[j], kt, mv,
                          bf(gmlp_w_out[j])).reshape(B, T, D)
        last = i == depth - 1
        x = _ffn(x.reshape(B * T, D), ffn2_norm[i], bf(ffn2_w_in[i]), bf(ffn2_w_out[i]),
                 final_norm if last else None).reshape(B, T, D)
    return x
```

```python
import functools

import jax
import jax.numpy as jnp
from jax import lax
from jax.experimental import pallas as pl
from jax.experimental.pallas import tpu as pltpu

EPS = 1e-6
HG_HEADS = 8
HG_CHUNK = 64
GM_CHUNK = 128
GM_GROUPS = 8
XA_HEADS = 4

V7X_VMEM_BYTES = 64 * 1024 * 1024
VMEM_LIMIT_BYTES = V7X_VMEM_BYTES - 8 * 1024 * 1024
MXU_COLS = 256
FFN_ROWS = 512
HGRN_ROWS = 512
GMLP_ROWS = 512

F32 = jnp.float32
BF16 = jnp.bfloat16


def _dot(a, b):
    return jnp.dot(a, b, preferred_element_type=F32)


def _dot_nt(a, b):
    return lax.dot_general(a, b, (((1,), (1,)), ((), ())), preferred_element_type=F32)


def _dot_tn(a, b):
    return lax.dot_general(a, b, (((0,), (0,)), ((), ())), preferred_element_type=F32)


def _rms(x, g):
    return x * lax.rsqrt(jnp.mean(x * x, axis=-1, keepdims=True) + EPS) * g


LOG2E = 1.4426950408889634


def _sigmoid(x):
    return 1.0 / (1.0 + jnp.exp2(x * (-LOG2E)))


def _silu(x):
    return x * _sigmoid(x)


def _gelu(x):
    return 0.5 * x * (1.0 + lax.erf(x * 0.7071067811865476))


def _const_spec(shape):
    nd = len(shape)
    return pl.BlockSpec(shape, lambda *_: (0,) * nd, pipeline_mode=pl.Buffered(1))


def _layer_spec(stack_shape, layer):
    nd = len(stack_shape)
    return pl.BlockSpec((1,) + tuple(stack_shape[1:]), lambda *_: (layer,) + (0,) * (nd - 1),
                        pipeline_mode=pl.Buffered(1))


def _params(n_grid):
    return pltpu.CompilerParams(
        dimension_semantics=("arbitrary",) * n_grid,
        vmem_limit_bytes=VMEM_LIMIT_BYTES,
    )


def _mem_kv_kernel(mem_ref, g_ref, wkv_ref, kt_ref, v_ref, *, scale):
    width = v_ref.shape[-1]
    mn = _rms(mem_ref[0], g_ref[...]).astype(BF16)
    kv = _dot(mn, wkv_ref[0])
    kt_ref[0, 0] = (kv[:, :width] * scale).T.astype(BF16)
    v_ref[0, 0] = kv[:, width:].astype(BF16)


def _mem_kv(mem, mem_norm, w_kv_bf16):
    B, M, D = mem.shape
    depth, _, two_w = w_kv_bf16.shape
    width = two_w // 2
    scale = float((width // XA_HEADS) ** -0.5)
    return pl.pallas_call(
        functools.partial(_mem_kv_kernel, scale=scale),
        out_shape=(jax.ShapeDtypeStruct((depth, B, width, M), BF16),
                   jax.ShapeDtypeStruct((depth, B, M, width), BF16)),
        grid=(depth, B),
        in_specs=[pl.BlockSpec((1, M, D), lambda l, b: (b, 0, 0)),
                  pl.BlockSpec((1, D), lambda l, b: (0, 0)),
                  pl.BlockSpec((1, D, two_w), lambda l, b: (l, 0, 0))],
        out_specs=(pl.BlockSpec((1, 1, width, M), lambda l, b: (l, b, 0, 0)),
                   pl.BlockSpec((1, 1, M, width), lambda l, b: (l, b, 0, 0))),
        compiler_params=_params(2),
        name="mem_kv",
    )(mem, mem_norm.reshape(1, D), w_kv_bf16)


def _ffn_kernel(x_ref, g_ref, win_ref, wout_ref, fg_ref, o_ref, xn_ref, acc_ref, *, layer, final_norm):
    d_ff = wout_ref.shape[1]
    x = x_ref[...]
    xn_ref[...] = _rms(x, g_ref[layer:layer + 1, :]).astype(BF16)
    for c in range(d_ff // MXU_COLS):
        lo = c * MXU_COLS
        xn = xn_ref[...]
        gate = _dot(xn, win_ref[0, :, lo:lo + MXU_COLS].astype(BF16))
        up = _dot(xn, win_ref[0, :, d_ff + lo:d_ff + lo + MXU_COLS].astype(BF16))
        act = (gate * _sigmoid(gate) * up).astype(BF16)
        part = _dot(act, wout_ref[0, lo:lo + MXU_COLS, :].astype(BF16))
        if c == 0:
            acc_ref[...] = part
        else:
            acc_ref[...] += part
    y = x_ref[...] + 0.5 * acc_ref[...]
    if final_norm:
        y = _rms(y, fg_ref[...])
    o_ref[...] = y


def _ffn(x2d, norm_stack, w_in_stack, w_out_stack, layer, final_g=None):
    N, D = x2d.shape
    d_ff = w_out_stack.shape[1]
    assert d_ff % MXU_COLS == 0 and N % FFN_ROWS == 0
    final_norm = final_g is not None
    fg = final_g.reshape(1, D) if final_norm else norm_stack[:1]
    return pl.pallas_call(
        functools.partial(_ffn_kernel, layer=layer, final_norm=final_norm),
        out_shape=jax.ShapeDtypeStruct((N, D), F32),
        grid=(N // FFN_ROWS,),
        in_specs=[pl.BlockSpec((FFN_ROWS, D), lambda i: (i, 0)),
                  _const_spec(norm_stack.shape),
                  _layer_spec(w_in_stack.shape, layer),
                  _layer_spec(w_out_stack.shape, layer),
                  _const_spec((1, D))],
        out_specs=pl.BlockSpec((FFN_ROWS, D), lambda i: (i, 0)),
        scratch_shapes=[pltpu.VMEM((FFN_ROWS, D), BF16),
                        pltpu.VMEM((FFN_ROWS, D), F32)],
        compiler_params=_params(1),
        name="ffn_final" if final_norm else "ffn",
    )(x2d, norm_stack, w_in_stack, w_out_stack, fg)


def _attn_scores(qx, kt_ref):
    xa_dim = qx.shape[1] // XA_HEADS
    probs = []
    for h in range(XA_HEADS):
        cols = slice(h * xa_dim, (h + 1) * xa_dim)
        s = _dot(qx[:, cols], kt_ref[0, 0, cols, :])
        p = jnp.exp(s - jnp.max(s, axis=-1, keepdims=True))
        probs.append((p.astype(BF16), jnp.sum(p, axis=-1, keepdims=True)))
    return probs


def _attn_values(probs, mv_ref):
    xa_dim = mv_ref.shape[-1] // XA_HEADS
    outs = []
    for h, (p, denom) in enumerate(probs):
        o = _dot(p, mv_ref[0, 0, :, h * xa_dim:(h + 1) * xa_dim]) / denom
        outs.append(o.astype(BF16))
    return outs


def _hgrn_kernel(x_ref, g_ref, lbl_ref, win_ref, gn_ref, kt_ref, mv_ref, wout_ref, o_ref,
                 h_ref, q_ref, k_ref, b_ref, v_ref, sg_ref, qd_ref, ki_ref, kr_ref, dec_ref,
                 a_ref, ut_ref, sp_ref, oh_ref, on_ref, st_ref, *, layer, slot):
    rows, D = h_ref.shape
    hd = D // HG_HEADS
    C = HG_CHUNK
    n_chunks = rows // C
    PW = MXU_COLS
    n_slabs = D // PW
    xa_dim = (win_ref.shape[2] - 4 * D) // XA_HEADS

    @pl.when(pl.program_id(1) == 0)
    def _():
        st_ref[...] = jnp.zeros_like(st_ref)

    h_ref[...] = _rms(x_ref[0], g_ref[layer:layer + 1, :]).astype(BF16)
    hb = h_ref[...]

    n_lb = lbl_ref.shape[0]
    lrows = [lbl_ref[r:r + 1, :] for r in range(n_lb)]
    lmax = functools.reduce(jnp.maximum, lrows)
    lexp = [jnp.exp(r - lmax) for r in lrows]
    lb = functools.reduce(lambda a, b: a + b, lexp[:layer + 1]) / functools.reduce(lambda a, b: a + b, lexp)

    R = min(rows, MXU_COLS)
    ri = lax.broadcasted_iota(jnp.int32, (R, R), 0)
    ci = lax.broadcasted_iota(jnp.int32, (R, R), 1)
    tril = jnp.where((ci <= ri) & (ci >= (ri // C) * C), 1.0, 0.0).astype(BF16)

    def forget_gate(p):
        cols = slice(p * PW, (p + 1) * PW)
        f = lb[:, cols] + (1.0 - lb[:, cols]) * _sigmoid(_dot(hb, win_ref[0, :, D + p * PW:D + (p + 1) * PW]))
        k_ref[:, cols] = 1.0 - f
        lf = jnp.log(f)
        lf_hi = lf.astype(BF16)
        rem = lf - lf_hi.astype(F32)
        lf_mid = rem.astype(BF16)
        lf_lo = (rem - lf_mid.astype(F32)).astype(BF16)
        return lf_hi, lf_mid, lf_lo

    split = forget_gate(0)
    for p in range(n_slabs):
        nxt = forget_gate(p + 1) if p + 1 < n_slabs else None
        for r in range(rows // R):
            rr = slice(r * R, (r + 1) * R)
            b_ref[rr, p * PW:(p + 1) * PW] = (_dot(tril, split[0][rr]) + _dot(tril, split[1][rr])
                                              + _dot(tril, split[2][rr])) * LOG2E
        split = nxt

    q_ref[...] = _silu(_dot(hb, win_ref[0, :, 0:D]))
    v_ref[...] = _dot(hb, win_ref[0, :, 2 * D:3 * D]).astype(BF16)

    for c in range(n_chunks):
        rs = slice(c * C, (c + 1) * C)
        b2 = b_ref[rs, :]
        bl2 = b_ref[(c + 1) * C - 1:(c + 1) * C, :]
        kk = k_ref[rs, :]
        qd_ref[rs, :] = (q_ref[rs, :] * jnp.exp2(b2)).astype(BF16)
        ki_ref[rs, :] = (kk * jnp.exp2(-b2)).astype(BF16)
        kr_ref[rs, :] = (kk * jnp.exp2(bl2 - b2)).astype(BF16)
        dec_ref[c:c + 1, :] = jnp.exp2(bl2)

    sg_ref[...] = _silu(_dot(hb, win_ref[0, :, 3 * D:4 * D]))
    probs = _attn_scores(_dot(hb, win_ref[0, :, 4 * D:]).astype(BF16), kt_ref)

    cr = lax.broadcasted_iota(jnp.int32, (C, C), 0)
    cc = lax.broadcasted_iota(jnp.int32, (C, C), 1)
    causal = cc <= cr
    for c in range(n_chunks):
        rs = slice(c * C, (c + 1) * C)
        for hh in range(HG_HEADS):
            cols = slice(hh * hd, (hh + 1) * hd)
            a = _dot_nt(qd_ref[rs, cols], ki_ref[rs, cols])
            a_ref[c * HG_HEADS + hh] = jnp.where(causal, a, 0.0).astype(BF16)
            ut_ref[c * HG_HEADS + hh] = _dot_tn(v_ref[rs, cols], kr_ref[rs, cols])

    omem = _attn_values(probs, mv_ref)

    def project_mem(h, first=False):
        part = _dot(omem[h], wout_ref[0, D + h * xa_dim:D + (h + 1) * xa_dim, :])
        if first:
            o_ref[0] = x_ref[0] + part
        else:
            o_ref[0] += part

    project_mem(0, first=True)
    project_mem(1)

    for hh in range(HG_HEADS):
        cols = slice(hh * hd, (hh + 1) * hd)
        st = st_ref[hh]
        for c in range(n_chunks):
            sp_ref[c * HG_HEADS + hh] = st.astype(BF16)
            st = st * dec_ref[c:c + 1, cols] + ut_ref[c * HG_HEADS + hh]
        st_ref[hh] = st

    for c in range(n_chunks):
        rs = slice(c * C, (c + 1) * C)
        for hh in range(HG_HEADS):
            cols = slice(hh * hd, (hh + 1) * hd)
            oh_ref[rs, cols] = (_dot(a_ref[c * HG_HEADS + hh], v_ref[rs, cols])
                                + _dot_nt(qd_ref[rs, cols], sp_ref[c * HG_HEADS + hh]))

    for h in range(2, XA_HEADS):
        project_mem(h)

    gn = gn_ref[slot:slot + 1, :]
    for hh in range(HG_HEADS):
        cols = slice(hh * hd, (hh + 1) * hd)
        o = oh_ref[:, cols]
        o = o * lax.rsqrt(jnp.mean(o * o, axis=-1, keepdims=True) + EPS) * gn
        on_ref[:, cols] = (o * sg_ref[:, cols]).astype(BF16)
    for p in range(n_slabs):
        o_ref[0] += _dot(on_ref[:, p * PW:(p + 1) * PW], wout_ref[0, p * PW:(p + 1) * PW, :])


def _hgrn_mix(x, norm_stack, lb_logits, layer, slot, w_in_bf16, gnorm_stack, kt, mv, w_out_bf16):
    B, T, D = x.shape
    rows = HGRN_ROWS
    hd = D // HG_HEADS
    M = kt.shape[-1]
    xw = kt.shape[-2]
    assert T % rows == 0 and rows % HG_CHUNK == 0 and D % MXU_COLS == 0
    n_ch = rows // HG_CHUNK
    return pl.pallas_call(
        functools.partial(_hgrn_kernel, layer=layer, slot=slot),
        out_shape=jax.ShapeDtypeStruct((B, T, D), F32),
        grid=(B, T // rows),
        in_specs=[pl.BlockSpec((1, rows, D), lambda b, t: (b, t, 0)),
                  _const_spec(norm_stack.shape),
                  _const_spec(lb_logits.shape),
                  _layer_spec(w_in_bf16.shape, slot),
                  _const_spec(gnorm_stack.shape),
                  pl.BlockSpec((1, 1, xw, M), lambda b, t: (layer, b, 0, 0)),
                  pl.BlockSpec((1, 1, M, xw), lambda b, t: (layer, b, 0, 0)),
                  _layer_spec(w_out_bf16.shape, slot)],
        out_specs=pl.BlockSpec((1, rows, D), lambda b, t: (b, t, 0)),
        scratch_shapes=[pltpu.VMEM((rows, D), BF16),
                        pltpu.VMEM((rows, D), F32),
                        pltpu.VMEM((rows, D), F32),
                        pltpu.VMEM((rows, D), F32),
                        pltpu.VMEM((rows, D), BF16),
                        pltpu.VMEM((rows, D), F32),
                        pltpu.VMEM((rows, D), BF16),
                        pltpu.VMEM((rows, D), BF16),
                        pltpu.VMEM((rows, D), BF16),
                        pltpu.VMEM((n_ch, D), F32),
                        pltpu.VMEM((n_ch * HG_HEADS, HG_CHUNK, HG_CHUNK), BF16),
                        pltpu.VMEM((n_ch * HG_HEADS, hd, hd), F32),
                        pltpu.VMEM((n_ch * HG_HEADS, hd, hd), BF16),
                        pltpu.VMEM((rows, D), F32),
                        pltpu.VMEM((rows, D), BF16),
                        pltpu.VMEM((HG_HEADS, hd, hd), F32)],
        compiler_params=_params(2),
        name="hgrn_mix",
    )(x, norm_stack, lb_logits, w_in_bf16, gnorm_stack, kt, mv, w_out_bf16)


def _gmlp_kernel(x_ref, g_ref, win_ref, lng_ref, lnb_ref, ws_ref, bs_ref, kt_ref, mv_ref, wout_ref, o_ref,
                 h_ref, v_ref, acc_ref, *, layer, slot):
    rows, D = x_ref.shape
    W = lng_ref.shape[1]
    C = GM_CHUNK
    gd = W // GM_GROUPS
    n_chunks = rows // C

    h_ref[...] = _rms(x_ref[...], g_ref[layer:layer + 1, :]).astype(BF16)
    hb = h_ref[...]

    gv = [_gelu(_dot(hb, win_ref[0, :, W + g * gd:W + (g + 1) * gd])) for g in range(GM_GROUPS)]
    mu = functools.reduce(lambda a, b: a + b, [jnp.sum(t, axis=-1, keepdims=True) for t in gv]) / W

    omem = _attn_values(_attn_scores(_dot(hb, win_ref[0, :, 2 * W:]).astype(BF16), kt_ref), mv_ref)
    xa_dim = (win_ref.shape[2] - 2 * W) // XA_HEADS
    acc_ref[...] = x_ref[...]
    for h in range(XA_HEADS):
        acc_ref[...] += _dot(omem[h], wout_ref[0, W + h * xa_dim:W + (h + 1) * xa_dim, :])

    xc = [t - mu for t in gv]
    var = functools.reduce(lambda a, b: a + b, [jnp.sum(t * t, axis=-1, keepdims=True) for t in xc]) / W
    rstd = lax.rsqrt(var + EPS)
    for g in range(GM_GROUPS):
        cols = slice(g * gd, (g + 1) * gd)
        v_ref[:, cols] = (xc[g] * rstd * lng_ref[slot:slot + 1, cols] + lnb_ref[slot:slot + 1, cols]).astype(BF16)

    ri = lax.broadcasted_iota(jnp.int32, (C, C), 0)
    ci = lax.broadcasted_iota(jnp.int32, (C, C), 1)
    causal = ci <= ri

    def project_u(g):
        return _gelu(_dot(h_ref[...], win_ref[0, :, g * gd:(g + 1) * gd]))

    def mix_and_project_out(g, u):
        cols = slice(g * gd, (g + 1) * gd)
        wm = jnp.where(causal, ws_ref[0, g], 0.0).astype(BF16)
        bias = bs_ref[0, :, g:g + 1]
        mixed = jnp.concatenate([_dot(wm, v_ref[c * C:(c + 1) * C, cols]) + bias for c in range(n_chunks)], axis=0)
        acc_ref[...] += _dot((u * mixed).astype(BF16), wout_ref[0, g * gd:(g + 1) * gd, :])

    u_next = project_u(0)
    for g in range(GM_GROUPS):
        u = u_next
        if g + 1 < GM_GROUPS:
            u_next = project_u(g + 1)
        mix_and_project_out(g, u)

    o_ref[...] = acc_ref[...]


def _gmlp_mix(x2d, seq_len, norm_stack, layer, slot, w_in_bf16, ln_g, ln_b, w_s, b_s_t, kt, mv, w_out_bf16):
    N, D = x2d.shape
    rows = GMLP_ROWS
    M = kt.shape[-1]
    xw = kt.shape[-2]
    assert seq_len % rows == 0 and rows % GM_CHUNK == 0
    per_seq = seq_len // rows
    return pl.pallas_call(
        functools.partial(_gmlp_kernel, layer=layer, slot=slot),
        out_shape=jax.ShapeDtypeStruct((N, D), F32),
        grid=(N // rows,),
        in_specs=[pl.BlockSpec((rows, D), lambda i: (i, 0)),
                  _const_spec(norm_stack.shape),
                  _layer_spec(w_in_bf16.shape, slot),
                  _const_spec(ln_g.shape),
                  _const_spec(ln_b.shape),
                  _layer_spec(w_s.shape, slot),
                  _layer_spec(b_s_t.shape, slot),
                  pl.BlockSpec((1, 1, xw, M), lambda i: (layer, i // per_seq, 0, 0)),
                  pl.BlockSpec((1, 1, M, xw), lambda i: (layer, i // per_seq, 0, 0)),
                  _layer_spec(w_out_bf16.shape, slot)],
        out_specs=pl.BlockSpec((rows, D), lambda i: (i, 0)),
        scratch_shapes=[pltpu.VMEM((rows, D), BF16),
                        pltpu.VMEM((rows, ln_g.shape[1]), BF16),
                        pltpu.VMEM((rows, D), F32)],
        compiler_params=_params(1),
        name="gmlp_mix",
    )(x2d, norm_stack, w_in_bf16, ln_g, ln_b, w_s, b_s_t, kt, mv, w_out_bf16)


def kernel(x, mem, mem_norm, lb_logits, ffn1_norm, ffn1_w_in, ffn1_w_out, mix_norm, mem_w_kv,
           hgrn_w_in, hgrn_gnorm, hgrn_w_out, gmlp_w_in, gmlp_ln_g, gmlp_ln_b, gmlp_w_s, gmlp_b_s,
           gmlp_w_out, ffn2_norm, ffn2_w_in, ffn2_w_out, final_norm):
    B, T, D = x.shape
    depth = ffn1_norm.shape[0]
    bf = lambda w: w.astype(BF16)

    kt, mv = _mem_kv(mem, mem_norm, bf(mem_w_kv))
    hgrn_w_in_b, hgrn_w_out_b = bf(hgrn_w_in), bf(hgrn_w_out)
    gmlp_w_in_b, gmlp_w_out_b = bf(gmlp_w_in), bf(gmlp_w_out)
    gmlp_b_s_t = jnp.swapaxes(gmlp_b_s, 1, 2)
    x = x.reshape(B * T, D)
    for i in range(depth):
        x = _ffn(x, ffn1_norm, ffn1_w_in, ffn1_w_out, i)
        j = i // 2
        if i % 2 == 0:
            x = _hgrn_mix(x.reshape(B, T, D), mix_norm, lb_logits, i, j, hgrn_w_in_b, hgrn_gnorm, kt, mv,
                          hgrn_w_out_b).reshape(B * T, D)
        else:
            x = _gmlp_mix(x, T, mix_norm, i, j, gmlp_w_in_b, gmlp_ln_g, gmlp_ln_b, gmlp_w_s, gmlp_b_s_t, kt, mv,
                          gmlp_w_out_b)
        x = _ffn(x, ffn2_norm, ffn2_w_in, ffn2_w_out, i, final_norm if i == depth - 1 else None)
    return x.reshape(B, T, D)
```

```python
import functools

import jax
import jax.numpy as jnp
from jax import lax
from jax.experimental import pallas as pl
from jax.experimental.pallas import tpu as pltpu

EPS = 1e-6
HG_HEADS = 8
HG_CHUNK = 64
GM_CHUNK = 128
GM_GROUPS = 8
XA_HEADS = 4

V7X_VMEM_BYTES = 64 * 1024 * 1024
VMEM_LIMIT_BYTES = V7X_VMEM_BYTES - 8 * 1024 * 1024
MXU_COLS = 256
FFN_ROWS = 512
HGRN_ROWS = 512
GMLP_ROWS = 512

F32 = jnp.float32
BF16 = jnp.bfloat16


def _dot(a, b):
    return jnp.dot(a, b, preferred_element_type=F32)


def _dot_nt(a, b):
    return lax.dot_general(a, b, (((1,), (1,)), ((), ())), preferred_element_type=F32)


def _dot_tn(a, b):
    return lax.dot_general(a, b, (((0,), (0,)), ((), ())), preferred_element_type=F32)


def _rms(x, g):
    return x * lax.rsqrt(jnp.mean(x * x, axis=-1, keepdims=True) + EPS) * g


LOG2E = 1.4426950408889634


def _sigmoid(x):
    return 1.0 / (1.0 + jnp.exp2(x * (-LOG2E)))


def _silu(x):
    return x * _sigmoid(x)


def _gelu(x):
    return 0.5 * x * (1.0 + lax.erf(x * 0.7071067811865476))


def _const_spec(shape):
    nd = len(shape)
    return pl.BlockSpec(shape, lambda *_: (0,) * nd, pipeline_mode=pl.Buffered(1))


def _layer_spec(stack_shape, layer):
    nd = len(stack_shape)
    return pl.BlockSpec((1,) + tuple(stack_shape[1:]), lambda *_: (layer,) + (0,) * (nd - 1),
                        pipeline_mode=pl.Buffered(1))


def _params(n_grid):
    return pltpu.CompilerParams(
        dimension_semantics=("arbitrary",) * n_grid,
        vmem_limit_bytes=VMEM_LIMIT_BYTES,
    )


def _mem_kv_kernel(mem_ref, g_ref, wkv_ref, kt_ref, v_ref, *, scale):
    width = v_ref.shape[-1]
    mn = _rms(mem_ref[0], g_ref[...]).astype(BF16)
    kv = _dot(mn, wkv_ref[0])
    kt_ref[0, 0] = (kv[:, :width] * scale).T.astype(BF16)
    v_ref[0, 0] = kv[:, width:].astype(BF16)


def _mem_kv(mem, mem_norm, w_kv_bf16):
    B, M, D = mem.shape
    depth, _, two_w = w_kv_bf16.shape
    width = two_w // 2
    scale = float((width // XA_HEADS) ** -0.5)
    return pl.pallas_call(
        functools.partial(_mem_kv_kernel, scale=scale),
        out_shape=(jax.ShapeDtypeStruct((depth, B, width, M), BF16),
                   jax.ShapeDtypeStruct((depth, B, M, width), BF16)),
        grid=(depth, B),
        in_specs=[pl.BlockSpec((1, M, D), lambda l, b: (b, 0, 0)),
                  pl.BlockSpec((1, D), lambda l, b: (0, 0)),
                  pl.BlockSpec((1, D, two_w), lambda l, b: (l, 0, 0))],
        out_specs=(pl.BlockSpec((1, 1, width, M), lambda l, b: (l, b, 0, 0)),
                   pl.BlockSpec((1, 1, M, width), lambda l, b: (l, b, 0, 0))),
        compiler_params=_params(2),
        name="mem_kv",
    )(mem, mem_norm.reshape(1, D), w_kv_bf16)


def _ffn_kernel(x_ref, g_ref, win_ref, wout_ref, fg_ref, o_ref, xn_ref, *, layer, final_norm):
    d_ff = wout_ref.shape[1]
    xn_ref[...] = _rms(x_ref[...], g_ref[layer:layer + 1, :]).astype(BF16)
    for c in range(d_ff // MXU_COLS):
        lo = c * MXU_COLS
        xn = xn_ref[...]
        gate = _dot(xn, win_ref[0, :, lo:lo + MXU_COLS].astype(BF16))
        up = _dot(xn, win_ref[0, :, d_ff + lo:d_ff + lo + MXU_COLS].astype(BF16))
        act = (gate * _sigmoid(gate) * (0.5 * up)).astype(BF16)
        part = _dot(act, wout_ref[0, lo:lo + MXU_COLS, :].astype(BF16))
        if c == 0:
            o_ref[...] = x_ref[...] + part
        else:
            o_ref[...] += part
    if final_norm:
        o_ref[...] = _rms(o_ref[...], fg_ref[...])


def _ffn(x2d, norm_stack, w_in_stack, w_out_stack, layer, final_g=None):
    N, D = x2d.shape
    d_ff = w_out_stack.shape[1]
    assert d_ff % MXU_COLS == 0 and N % FFN_ROWS == 0
    final_norm = final_g is not None
    fg = final_g.reshape(1, D) if final_norm else norm_stack[:1]
    return pl.pallas_call(
        functools.partial(_ffn_kernel, layer=layer, final_norm=final_norm),
        out_shape=jax.ShapeDtypeStruct((N, D), F32),
        grid=(N // FFN_ROWS,),
        in_specs=[pl.BlockSpec((FFN_ROWS, D), lambda i: (i, 0)),
                  _const_spec(norm_stack.shape),
                  _layer_spec(w_in_stack.shape, layer),
                  _layer_spec(w_out_stack.shape, layer),
                  _const_spec((1, D))],
        out_specs=pl.BlockSpec((FFN_ROWS, D), lambda i: (i, 0)),
        scratch_shapes=[pltpu.VMEM((FFN_ROWS, D), BF16)],
        compiler_params=_params(1),
        name="ffn_final" if final_norm else "ffn",
    )(x2d, norm_stack, w_in_stack, w_out_stack, fg)


def _attn_scores(qx, kt_ref):
    xa_dim = qx.shape[1] // XA_HEADS
    probs = []
    for h in range(XA_HEADS):
        cols = slice(h * xa_dim, (h + 1) * xa_dim)
        s = _dot(qx[:, cols], kt_ref[0, 0, cols, :])
        p = jnp.exp(s - jnp.max(s, axis=-1, keepdims=True))
        probs.append((p.astype(BF16), jnp.sum(p, axis=-1, keepdims=True)))
    return probs


def _attn_values(probs, mv_ref):
    xa_dim = mv_ref.shape[-1] // XA_HEADS
    outs = []
    for h, (p, denom) in enumerate(probs):
        o = _dot(p, mv_ref[0, 0, :, h * xa_dim:(h + 1) * xa_dim]) / denom
        outs.append(o.astype(BF16))
    return outs


def _hgrn_kernel(x_ref, g_ref, lbl_ref, win_ref, gn_ref, kt_ref, mv_ref, wout_ref, o_ref,
                 h_ref, q_ref, k_ref, b_ref, v_ref, sg_ref, qx_ref, qd_ref, ki_ref, kr_ref, dec_ref,
                 a_ref, ut_ref, sp_ref, oh_ref, on_ref, st_ref, *, layer, slot):
    rows, D = h_ref.shape
    hd = D // HG_HEADS
    C = HG_CHUNK
    n_chunks = rows // C
    PW = MXU_COLS
    n_slabs = D // PW
    xa_dim = (win_ref.shape[2] - 4 * D) // XA_HEADS

    @pl.when(pl.program_id(1) == 0)
    def _():
        st_ref[...] = jnp.zeros_like(st_ref)

    h_ref[...] = _rms(x_ref[0], g_ref[layer:layer + 1, :]).astype(BF16)
    hb = h_ref[...]

    n_lb = lbl_ref.shape[0]
    lrows = [lbl_ref[r:r + 1, :] for r in range(n_lb)]
    lmax = functools.reduce(jnp.maximum, lrows)
    lexp = [jnp.exp(r - lmax) for r in lrows]
    lb = functools.reduce(lambda a, b: a + b, lexp[:layer + 1]) / functools.reduce(lambda a, b: a + b, lexp)

    R = min(rows, MXU_COLS)
    ri = lax.broadcasted_iota(jnp.int32, (R, R), 0)
    ci = lax.broadcasted_iota(jnp.int32, (R, R), 1)
    tril = jnp.where((ci <= ri) & (ci >= (ri // C) * C), 1.0, 0.0).astype(BF16)

    def forget_gate(p):
        cols = slice(p * PW, (p + 1) * PW)
        f = lb[:, cols] + (1.0 - lb[:, cols]) * _sigmoid(_dot(hb, win_ref[0, :, D + p * PW:D + (p + 1) * PW]))
        k_ref[:, cols] = 1.0 - f
        lf = jnp.log(f)
        lf_hi = lf.astype(BF16)
        rem = lf - lf_hi.astype(F32)
        lf_mid = rem.astype(BF16)
        lf_lo = (rem - lf_mid.astype(F32)).astype(BF16)
        return lf_hi, lf_mid, lf_lo

    split = forget_gate(0)
    for p in range(n_slabs):
        nxt = forget_gate(p + 1) if p + 1 < n_slabs else None
        q_ref[:, p * PW:(p + 1) * PW] = _silu(_dot(hb, win_ref[0, :, p * PW:(p + 1) * PW]))
        for r in range(rows // R):
            rr = slice(r * R, (r + 1) * R)
            b_ref[rr, p * PW:(p + 1) * PW] = (_dot(tril, split[0][rr]) + _dot(tril, split[1][rr])
                                              + _dot(tril, split[2][rr])) * LOG2E
        split = nxt

    v_ref[...] = _dot(hb, win_ref[0, :, 2 * D:3 * D]).astype(BF16)

    cr = lax.broadcasted_iota(jnp.int32, (C, C), 0)
    cc = lax.broadcasted_iota(jnp.int32, (C, C), 1)
    causal = cc <= cr
    gn = gn_ref[slot:slot + 1, :]

    def decay(c):
        rs = slice(c * C, (c + 1) * C)
        b2 = b_ref[rs, :]
        bl2 = b_ref[(c + 1) * C - 1:(c + 1) * C, :]
        kk = k_ref[rs, :]
        qd_ref[rs, :] = (q_ref[rs, :] * jnp.exp2(b2)).astype(BF16)
        ki_ref[rs, :] = (kk * jnp.exp2(-b2)).astype(BF16)
        kr_ref[rs, :] = (kk * jnp.exp2(bl2 - b2)).astype(BF16)
        dec_ref[c:c + 1, :] = jnp.exp2(bl2)

    def side_projection(i):
        p = i % n_slabs
        cols = slice(p * PW, (p + 1) * PW)
        if i < n_slabs:
            sg_ref[:, cols] = _silu(_dot(hb, win_ref[0, :, 3 * D + p * PW:3 * D + (p + 1) * PW]))
        else:
            qx_ref[:, cols] = _dot(hb, win_ref[0, :, 4 * D + p * PW:4 * D + (p + 1) * PW]).astype(BF16)

    def chunk_scores(c):
        rs = slice(c * C, (c + 1) * C)
        for hh in range(HG_HEADS):
            cols = slice(hh * hd, (hh + 1) * hd)
            a = _dot_nt(qd_ref[rs, cols], ki_ref[rs, cols])
            a_ref[c * HG_HEADS + hh] = jnp.where(causal, a, 0.0).astype(BF16)
            ut_ref[c * HG_HEADS + hh] = _dot_tn(v_ref[rs, cols], kr_ref[rs, cols])

    n_side = 2 * n_slabs
    per_chunk = -(-n_side // n_chunks)
    decay(0)
    for c in range(n_chunks):
        for i in range(c * per_chunk, min((c + 1) * per_chunk, n_side)):
            side_projection(i)
        if c + 1 < n_chunks:
            decay(c + 1)
        chunk_scores(c)

    probs = _attn_scores(qx_ref[...], kt_ref)

    def scan(c):
        for hh in range(HG_HEADS):
            cols = slice(hh * hd, (hh + 1) * hd)
            st = st_ref[hh]
            sp_ref[c * HG_HEADS + hh] = st.astype(BF16).T
            st_ref[hh] = st * dec_ref[c:c + 1, cols] + ut_ref[c * HG_HEADS + hh]

    def chunk_outputs(c):
        rs = slice(c * C, (c + 1) * C)
        for hh in range(HG_HEADS):
            cols = slice(hh * hd, (hh + 1) * hd)
            oh_ref[rs, cols] = (_dot(a_ref[c * HG_HEADS + hh], v_ref[rs, cols])
                                + _dot(qd_ref[rs, cols], sp_ref[c * HG_HEADS + hh]))

    def chunk_norm(c):
        rs = slice(c * C, (c + 1) * C)
        for hh in range(HG_HEADS):
            cols = slice(hh * hd, (hh + 1) * hd)
            o = oh_ref[rs, cols]
            o = o * lax.rsqrt(jnp.mean(o * o, axis=-1, keepdims=True) + EPS) * gn
            on_ref[rs, cols] = (o * sg_ref[rs, cols]).astype(BF16)

    def memory_head(h):
        p, denom = probs[h]
        o = (_dot(p, mv_ref[0, 0, :, h * xa_dim:(h + 1) * xa_dim]) / denom).astype(BF16)
        part = _dot(o, wout_ref[0, D + h * xa_dim:D + (h + 1) * xa_dim, :])
        if h == 0:
            o_ref[0] = x_ref[0] + part
        else:
            o_ref[0] += part

    heads_done = 0
    for c in range(n_chunks):
        scan(c)
        want = -(-(c + 1) * XA_HEADS // n_chunks)
        while heads_done < want:
            memory_head(heads_done)
            heads_done += 1
        chunk_outputs(c)
        if c >= 1:
            chunk_norm(c - 1)
    chunk_norm(n_chunks - 1)

    for p in range(n_slabs):
        o_ref[0] += _dot(on_ref[:, p * PW:(p + 1) * PW], wout_ref[0, p * PW:(p + 1) * PW, :])


def _hgrn_mix(x, norm_stack, lb_logits, layer, slot, w_in_bf16, gnorm_stack, kt, mv, w_out_bf16):
    B, T, D = x.shape
    rows = HGRN_ROWS
    hd = D // HG_HEADS
    M = kt.shape[-1]
    xw = kt.shape[-2]
    assert T % rows == 0 and rows % HG_CHUNK == 0 and D % MXU_COLS == 0
    n_ch = rows // HG_CHUNK
    return pl.pallas_call(
        functools.partial(_hgrn_kernel, layer=layer, slot=slot),
        out_shape=jax.ShapeDtypeStruct((B, T, D), F32),
        grid=(B, T // rows),
        in_specs=[pl.BlockSpec((1, rows, D), lambda b, t: (b, t, 0)),
                  _const_spec(norm_stack.shape),
                  _const_spec(lb_logits.shape),
                  _layer_spec(w_in_bf16.shape, slot),
                  _const_spec(gnorm_stack.shape),
                  pl.BlockSpec((1, 1, xw, M), lambda b, t: (layer, b, 0, 0)),
                  pl.BlockSpec((1, 1, M, xw), lambda b, t: (layer, b, 0, 0)),
                  _layer_spec(w_out_bf16.shape, slot)],
        out_specs=pl.BlockSpec((1, rows, D), lambda b, t: (b, t, 0)),
        scratch_shapes=[pltpu.VMEM((rows, D), BF16),
                        pltpu.VMEM((rows, D), F32),
                        pltpu.VMEM((rows, D), F32),
                        pltpu.VMEM((rows, D), F32),
                        pltpu.VMEM((rows, D), BF16),
                        pltpu.VMEM((rows, D), F32),
                        pltpu.VMEM((rows, xw), BF16),
                        pltpu.VMEM((rows, D), BF16),
                        pltpu.VMEM((rows, D), BF16),
                        pltpu.VMEM((rows, D), BF16),
                        pltpu.VMEM((n_ch, D), F32),
                        pltpu.VMEM((n_ch * HG_HEADS, HG_CHUNK, HG_CHUNK), BF16),
                        pltpu.VMEM((n_ch * HG_HEADS, hd, hd), F32),
                        pltpu.VMEM((n_ch * HG_HEADS, hd, hd), BF16),
                        pltpu.VMEM((rows, D), F32),
                        pltpu.VMEM((rows, D), BF16),
                        pltpu.VMEM((HG_HEADS, hd, hd), F32)],
        compiler_params=_params(2),
        name="hgrn_mix",
    )(x, norm_stack, lb_logits, w_in_bf16, gnorm_stack, kt, mv, w_out_bf16)


def _gmlp_kernel(x_ref, g_ref, win_ref, lng_ref, lnb_ref, ws_ref, bs_ref, kt_ref, mv_ref, wout_ref, o_ref,
                 h_ref, v_ref, acc_ref, *, layer, slot):
    rows, D = x_ref.shape
    W = lng_ref.shape[1]
    C = GM_CHUNK
    gd = W // GM_GROUPS
    n_chunks = rows // C

    h_ref[...] = _rms(x_ref[...], g_ref[layer:layer + 1, :]).astype(BF16)
    hb = h_ref[...]

    gv = [_gelu(_dot(hb, win_ref[0, :, W + g * gd:W + (g + 1) * gd])) for g in range(GM_GROUPS)]
    mu = functools.reduce(lambda a, b: a + b, [jnp.sum(t, axis=-1, keepdims=True) for t in gv]) / W

    omem = _attn_values(_attn_scores(_dot(hb, win_ref[0, :, 2 * W:]).astype(BF16), kt_ref), mv_ref)
    xa_dim = (win_ref.shape[2] - 2 * W) // XA_HEADS
    acc_ref[...] = x_ref[...]
    for h in range(XA_HEADS):
        acc_ref[...] += _dot(omem[h], wout_ref[0, W + h * xa_dim:W + (h + 1) * xa_dim, :])

    xc = [t - mu for t in gv]
    var = functools.reduce(lambda a, b: a + b, [jnp.sum(t * t, axis=-1, keepdims=True) for t in xc]) / W
    rstd = lax.rsqrt(var + EPS)
    for g in range(GM_GROUPS):
        cols = slice(g * gd, (g + 1) * gd)
        v_ref[:, cols] = (xc[g] * rstd * lng_ref[slot:slot + 1, cols] + lnb_ref[slot:slot + 1, cols]).astype(BF16)

    ri = lax.broadcasted_iota(jnp.int32, (C, C), 0)
    ci = lax.broadcasted_iota(jnp.int32, (C, C), 1)
    causal = ci <= ri

    def project_u(g):
        return _gelu(_dot(h_ref[...], win_ref[0, :, g * gd:(g + 1) * gd]))

    def mix_and_project_out(g, u):
        cols = slice(g * gd, (g + 1) * gd)
        wm = jnp.where(causal, ws_ref[0, g], 0.0).astype(BF16)
        bias = bs_ref[0, :, g:g + 1]
        mixed = jnp.concatenate([_dot(wm, v_ref[c * C:(c + 1) * C, cols]) + bias for c in range(n_chunks)], axis=0)
        acc_ref[...] += _dot((u * mixed).astype(BF16), wout_ref[0, g * gd:(g + 1) * gd, :])

    u_next = project_u(0)
    for g in range(GM_GROUPS):
        u = u_next
        if g + 1 < GM_GROUPS:
            u_next = project_u(g + 1)
        mix_and_project_out(g, u)

    o_ref[...] = acc_ref[...]


def _gmlp_mix(x2d, seq_len, norm_stack, layer, slot, w_in_bf16, ln_g, ln_b, w_s, b_s_t, kt, mv, w_out_bf16):
    N, D = x2d.shape
    rows = GMLP_ROWS
    M = kt.shape[-1]
    xw = kt.shape[-2]
    assert seq_len % rows == 0 and rows % GM_CHUNK == 0
    per_seq = seq_len // rows
    return pl.pallas_call(
        functools.partial(_gmlp_kernel, layer=layer, slot=slot),
        out_shape=jax.ShapeDtypeStruct((N, D), F32),
        grid=(N // rows,),
        in_specs=[pl.BlockSpec((rows, D), lambda i: (i, 0)),
                  _const_spec(norm_stack.shape),
                  _layer_spec(w_in_bf16.shape, slot),
                  _const_spec(ln_g.shape),
                  _const_spec(ln_b.shape),
                  _layer_spec(w_s.shape, slot),
                  _layer_spec(b_s_t.shape, slot),
                  pl.BlockSpec((1, 1, xw, M), lambda i: (layer, i // per_seq, 0, 0)),
                  pl.BlockSpec((1, 1, M, xw), lambda i: (layer, i // per_seq, 0, 0)),
                  _layer_spec(w_out_bf16.shape, slot)],
        out_specs=pl.BlockSpec((rows, D), lambda i: (i, 0)),
        scratch_shapes=[pltpu.VMEM((rows, D), BF16),
                        pltpu.VMEM((rows, ln_g.shape[1]), BF16),
                        pltpu.VMEM((rows, D), F32)],
        compiler_params=_params(1),
        name="gmlp_mix",
    )(x2d, norm_stack, w_in_bf16, ln_g, ln_b, w_s, b_s_t, kt, mv, w_out_bf16)


def kernel(x, mem, mem_norm, lb_logits, ffn1_norm, ffn1_w_in, ffn1_w_out, mix_norm, mem_w_kv,
           hgrn_w_in, hgrn_gnorm, hgrn_w_out, gmlp_w_in, gmlp_ln_g, gmlp_ln_b, gmlp_w_s, gmlp_b_s,
           gmlp_w_out, ffn2_norm, ffn2_w_in, ffn2_w_out, final_norm):
    B, T, D = x.shape
    depth = ffn1_norm.shape[0]
    bf = lambda w: w.astype(BF16)

    kt, mv = _mem_kv(mem, mem_norm, bf(mem_w_kv))
    hgrn_w_in_b, hgrn_w_out_b = bf(hgrn_w_in), bf(hgrn_w_out)
    gmlp_w_in_b, gmlp_w_out_b = bf(gmlp_w_in), bf(gmlp_w_out)
    gmlp_b_s_t = jnp.swapaxes(gmlp_b_s, 1, 2)
    x = x.reshape(B * T, D)
    for i in range(depth):
        x = _ffn(x, ffn1_norm, ffn1_w_in, ffn1_w_out, i)
        j = i // 2
        if i % 2 == 0:
            x = _hgrn_mix(x.reshape(B, T, D), mix_norm, lb_logits, i, j, hgrn_w_in_b, hgrn_gnorm, kt, mv,
                          hgrn_w_out_b).reshape(B * T, D)
        else:
            x = _gmlp_mix(x, T, mix_norm, i, j, gmlp_w_in_b, gmlp_ln_g, gmlp_ln_b, gmlp_w_s, gmlp_b_s_t, kt, mv,
                          gmlp_w_out_b)
        x = _ffn(x, ffn2_norm, ffn2_w_in, ffn2_w_out, i, final_norm if i == depth - 1 else None)
    return x.reshape(B, T, D)
```

```python
import functools

import jax
import jax.numpy as jnp
from jax import lax
from jax.experimental import pallas as pl
from jax.experimental.pallas import tpu as pltpu

EPS = 1e-6
HG_HEADS = 8
HG_CHUNK = 64
GM_CHUNK = 128
GM_GROUPS = 8
XA_HEADS = 4

V7X_VMEM_BYTES = 64 * 1024 * 1024
VMEM_LIMIT_BYTES = V7X_VMEM_BYTES - 5 * 1024 * 1024
MXU_COLS = 256
FFN_ROWS = 512
HGRN_ROWS = 512
GMLP_ROWS = 512
MEM_KV_BATCH = 4

F32 = jnp.float32
BF16 = jnp.bfloat16


def _dot(a, b):
    return jnp.dot(a, b, preferred_element_type=F32)


def _dot_nt(a, b):
    return lax.dot_general(a, b, (((1,), (1,)), ((), ())), preferred_element_type=F32)


def _dot_tn(a, b):
    return lax.dot_general(a, b, (((0,), (0,)), ((), ())), preferred_element_type=F32)


def _rms(x, g):
    return x * lax.rsqrt(jnp.mean(x * x, axis=-1, keepdims=True) + EPS) * g


LOG2E = 1.4426950408889634


def _sigmoid(x):
    return 1.0 / (1.0 + jnp.exp2(x * (-LOG2E)))


def _silu(x):
    return x * _sigmoid(x)


def _gelu(x):
    return 0.5 * x * (1.0 + lax.erf(x * 0.7071067811865476))


def _const_spec(shape):
    nd = len(shape)
    return pl.BlockSpec(shape, lambda *_: (0,) * nd, pipeline_mode=pl.Buffered(1))


def _layer_spec(stack_shape, layer):
    nd = len(stack_shape)
    return pl.BlockSpec((1,) + tuple(stack_shape[1:]), lambda *_: (layer,) + (0,) * (nd - 1),
                        pipeline_mode=pl.Buffered(1))


def _params(n_grid):
    return pltpu.CompilerParams(
        dimension_semantics=("arbitrary",) * n_grid,
        vmem_limit_bytes=VMEM_LIMIT_BYTES,
    )


def _mem_kv_kernel(mem_ref, g_ref, wkv_ref, kt_ref, v_ref, *, scale):
    nb, M, D = mem_ref.shape
    width = v_ref.shape[-1]
    mn = _rms(mem_ref[...].reshape(nb * M, D), g_ref[...]).astype(BF16)
    kv = _dot(mn, wkv_ref[0].astype(BF16))
    for j in range(nb):
        rs = slice(j * M, (j + 1) * M)
        kt_ref[0, j] = (kv[rs, :width] * scale).T.astype(BF16)
        v_ref[0, j] = kv[rs, width:].astype(BF16)


def _mem_kv(mem, mem_norm, w_kv):
    B, M, D = mem.shape
    depth, _, two_w = w_kv.shape
    width = two_w // 2
    scale = float((width // XA_HEADS) ** -0.5)
    nb = MEM_KV_BATCH
    assert B % nb == 0
    return pl.pallas_call(
        functools.partial(_mem_kv_kernel, scale=scale),
        out_shape=(jax.ShapeDtypeStruct((depth, B, width, M), BF16),
                   jax.ShapeDtypeStruct((depth, B, M, width), BF16)),
        grid=(depth, B // nb),
        in_specs=[pl.BlockSpec((nb, M, D), lambda l, b: (b, 0, 0)),
                  pl.BlockSpec((1, D), lambda l, b: (0, 0)),
                  pl.BlockSpec((1, D, two_w), lambda l, b: (l, 0, 0))],
        out_specs=(pl.BlockSpec((1, nb, width, M), lambda l, b: (l, b, 0, 0)),
                   pl.BlockSpec((1, nb, M, width), lambda l, b: (l, b, 0, 0))),
        compiler_params=_params(2),
        name="mem_kv",
    )(mem, mem_norm.reshape(1, D), w_kv)


def _ffn_kernel(x_ref, g_ref, win_ref, wout_ref, fg_ref, o_ref, xn_ref, *, layer, final_norm):
    d_ff = wout_ref.shape[1]
    xn_ref[...] = _rms(x_ref[...], g_ref[layer:layer + 1, :]).astype(BF16)
    for c in range(d_ff // MXU_COLS):
        lo = c * MXU_COLS
        xn = xn_ref[...]
        gate = _dot(xn, win_ref[0, :, lo:lo + MXU_COLS].astype(BF16))
        up = _dot(xn, win_ref[0, :, d_ff + lo:d_ff + lo + MXU_COLS].astype(BF16))
        act = (gate * _sigmoid(gate) * (0.5 * up)).astype(BF16)
        part = _dot(act, wout_ref[0, lo:lo + MXU_COLS, :].astype(BF16))
        if c == 0:
            o_ref[...] = x_ref[...] + part
        else:
            o_ref[...] += part
    if final_norm:
        o_ref[...] = _rms(o_ref[...], fg_ref[...])


def _ffn(x2d, norm_stack, w_in_stack, w_out_stack, layer, final_g=None):
    N, D = x2d.shape
    d_ff = w_out_stack.shape[1]
    assert d_ff % MXU_COLS == 0 and N % FFN_ROWS == 0
    final_norm = final_g is not None
    fg = final_g.reshape(1, D) if final_norm else norm_stack[:1]
    return pl.pallas_call(
        functools.partial(_ffn_kernel, layer=layer, final_norm=final_norm),
        out_shape=jax.ShapeDtypeStruct((N, D), F32),
        grid=(N // FFN_ROWS,),
        in_specs=[pl.BlockSpec((FFN_ROWS, D), lambda i: (i, 0)),
                  _const_spec(norm_stack.shape),
                  _layer_spec(w_in_stack.shape, layer),
                  _layer_spec(w_out_stack.shape, layer),
                  _const_spec((1, D))],
        out_specs=pl.BlockSpec((FFN_ROWS, D), lambda i: (i, 0)),
        scratch_shapes=[pltpu.VMEM((FFN_ROWS, D), BF16)],
        compiler_params=_params(1),
        name="ffn_final" if final_norm else "ffn",
    )(x2d, norm_stack, w_in_stack, w_out_stack, fg)


def _attn_scores(qx, kt_ref):
    xa_dim = qx.shape[1] // XA_HEADS
    probs = []
    for h in range(XA_HEADS):
        cols = slice(h * xa_dim, (h + 1) * xa_dim)
        s = _dot(qx[:, cols], kt_ref[0, 0, cols, :])
        p = jnp.exp(s - jnp.max(s, axis=-1, keepdims=True))
        probs.append((p.astype(BF16), jnp.sum(p, axis=-1, keepdims=True)))
    return probs


def _attn_values(probs, mv_ref):
    xa_dim = mv_ref.shape[-1] // XA_HEADS
    outs = []
    for h, (p, denom) in enumerate(probs):
        o = _dot(p, mv_ref[0, 0, :, h * xa_dim:(h + 1) * xa_dim]) / denom
        outs.append(o.astype(BF16))
    return outs


def _hgrn_kernel(x_ref, g_ref, lbl_ref, win_ref, gn_ref, kt_ref, mv_ref, wout_ref, o_ref,
                 h_ref, q_ref, k_ref, b_ref, v_ref, sg_ref, qx_ref, qd_ref, ki_ref, kr_ref, dec_ref,
                 a_ref, ut_ref, sp_ref, oh_ref, on_ref, st_ref, *, layer, slot):
    rows, D = h_ref.shape
    hd = D // HG_HEADS
    C = HG_CHUNK
    n_chunks = rows // C
    PW = MXU_COLS
    n_slabs = D // PW
    xa_dim = (win_ref.shape[2] - 4 * D) // XA_HEADS

    @pl.when(pl.program_id(1) == 0)
    def _():
        st_ref[...] = jnp.zeros_like(st_ref)

    h_ref[...] = _rms(x_ref[0], g_ref[layer:layer + 1, :]).astype(BF16)
    hb = h_ref[...]

    def w_in(col0):
        return win_ref[0, :, col0:col0 + PW].astype(BF16)

    def w_out(row0, n_rows):
        return wout_ref[0, row0:row0 + n_rows, :].astype(BF16)

    n_lb = lbl_ref.shape[0]
    lrows = [lbl_ref[r:r + 1, :] for r in range(n_lb)]
    lmax = functools.reduce(jnp.maximum, lrows)
    lexp = [jnp.exp(r - lmax) for r in lrows]
    lb = functools.reduce(lambda a, b: a + b, lexp[:layer + 1]) / functools.reduce(lambda a, b: a + b, lexp)

    R = min(rows, MXU_COLS)
    ri = lax.broadcasted_iota(jnp.int32, (R, R), 0)
    ci = lax.broadcasted_iota(jnp.int32, (R, R), 1)
    tril = jnp.where((ci <= ri) & (ci >= (ri // C) * C), 1.0, 0.0).astype(BF16)

    def forget_gate(p):
        cols = slice(p * PW, (p + 1) * PW)
        f = lb[:, cols] + (1.0 - lb[:, cols]) * _sigmoid(_dot(hb, w_in(D + p * PW)))
        k_ref[:, cols] = 1.0 - f
        lf = jnp.log(f)
        lf_hi = lf.astype(BF16)
        rem = lf - lf_hi.astype(F32)
        lf_mid = rem.astype(BF16)
        lf_lo = (rem - lf_mid.astype(F32)).astype(BF16)
        return lf_hi, lf_mid, lf_lo

    split = forget_gate(0)
    for p in range(n_slabs):
        nxt = forget_gate(p + 1) if p + 1 < n_slabs else None
        q_ref[:, p * PW:(p + 1) * PW] = _silu(_dot(hb, w_in(p * PW)))
        for r in range(rows // R):
            rr = slice(r * R, (r + 1) * R)
            b_ref[rr, p * PW:(p + 1) * PW] = (_dot(tril, split[0][rr]) + _dot(tril, split[1][rr])
                                              + _dot(tril, split[2][rr])) * LOG2E
        split = nxt

    for p in range(n_slabs):
        v_ref[:, p * PW:(p + 1) * PW] = _dot(hb, w_in(2 * D + p * PW)).astype(BF16)

    cr = lax.broadcasted_iota(jnp.int32, (C, C), 0)
    cc = lax.broadcasted_iota(jnp.int32, (C, C), 1)
    causal = cc <= cr
    gn = gn_ref[slot:slot + 1, :]

    def decay(c):
        rs = slice(c * C, (c + 1) * C)
        b2 = b_ref[rs, :]
        bl2 = b_ref[(c + 1) * C - 1:(c + 1) * C, :]
        kk = k_ref[rs, :]
        qd_ref[rs, :] = (q_ref[rs, :] * jnp.exp2(b2)).astype(BF16)
        ki_ref[rs, :] = (kk * jnp.exp2(-b2)).astype(BF16)
        kr_ref[rs, :] = (kk * jnp.exp2(bl2 - b2)).astype(BF16)
        dec_ref[c:c + 1, :] = jnp.exp2(bl2)

    def side_projection(i):
        p = i % n_slabs
        cols = slice(p * PW, (p + 1) * PW)
        if i < n_slabs:
            sg_ref[:, cols] = _silu(_dot(hb, w_in(3 * D + p * PW)))
        else:
            qx_ref[:, cols] = _dot(hb, w_in(4 * D + p * PW)).astype(BF16)

    def chunk_scores(c):
        rs = slice(c * C, (c + 1) * C)
        for hh in range(HG_HEADS):
            cols = slice(hh * hd, (hh + 1) * hd)
            a = _dot_nt(qd_ref[rs, cols], ki_ref[rs, cols])
            a_ref[c * HG_HEADS + hh] = jnp.where(causal, a, 0.0).astype(BF16)
            ut_ref[c * HG_HEADS + hh] = _dot_tn(v_ref[rs, cols], kr_ref[rs, cols])

    n_side = 2 * n_slabs
    per_chunk = -(-n_side // n_chunks)
    decay(0)
    for c in range(n_chunks):
        for i in range(c * per_chunk, min((c + 1) * per_chunk, n_side)):
            side_projection(i)
        if c + 1 < n_chunks:
            decay(c + 1)
        chunk_scores(c)

    probs = _attn_scores(qx_ref[...], kt_ref)

    def scan(c):
        for hh in range(HG_HEADS):
            cols = slice(hh * hd, (hh + 1) * hd)
            st = st_ref[hh]
            sp_ref[c * HG_HEADS + hh] = st.astype(BF16).T
            st_ref[hh] = st * dec_ref[c:c + 1, cols] + ut_ref[c * HG_HEADS + hh]

    def chunk_outputs(c):
        rs = slice(c * C, (c + 1) * C)
        for hh in range(HG_HEADS):
            cols = slice(hh * hd, (hh + 1) * hd)
            lhs = jnp.concatenate([qd_ref[rs, cols], a_ref[c * HG_HEADS + hh]], axis=1)
            rhs = jnp.concatenate([sp_ref[c * HG_HEADS + hh], v_ref[rs, cols]], axis=0)
            oh_ref[rs, cols] = _dot(lhs, rhs)

    def chunk_norm(c):
        rs = slice(c * C, (c + 1) * C)
        for hh in range(HG_HEADS):
            cols = slice(hh * hd, (hh + 1) * hd)
            o = oh_ref[rs, cols]
            o = o * lax.rsqrt(jnp.mean(o * o, axis=-1, keepdims=True) + EPS) * gn
            on_ref[rs, cols] = (o * sg_ref[rs, cols]).astype(BF16)

    def memory_head(h):
        p, denom = probs[h]
        o = (_dot(p, mv_ref[0, 0, :, h * xa_dim:(h + 1) * xa_dim]) / denom).astype(BF16)
        part = _dot(o, w_out(D + h * xa_dim, xa_dim))
        if h == 0:
            o_ref[0] = x_ref[0] + part
        else:
            o_ref[0] += part

    heads_done = 0
    for c in range(n_chunks):
        scan(c)
        want = -(-(c + 1) * XA_HEADS // n_chunks)
        while heads_done < want:
            memory_head(heads_done)
            heads_done += 1
        chunk_outputs(c)
        if c >= 1:
            chunk_norm(c - 1)
    chunk_norm(n_chunks - 1)

    for p in range(n_slabs):
        o_ref[0] += _dot(on_ref[:, p * PW:(p + 1) * PW], w_out(p * PW, PW))


def _hgrn_mix(x, norm_stack, lb_logits, layer, slot, w_in_stack, gnorm_stack, kt, mv, w_out_stack):
    B, T, D = x.shape
    rows = HGRN_ROWS
    hd = D // HG_HEADS
    M = kt.shape[-1]
    xw = kt.shape[-2]
    assert T % rows == 0 and rows % HG_CHUNK == 0 and D % MXU_COLS == 0
    n_ch = rows // HG_CHUNK
    return pl.pallas_call(
        functools.partial(_hgrn_kernel, layer=layer, slot=slot),
        out_shape=jax.ShapeDtypeStruct((B, T, D), F32),
        grid=(B, T // rows),
        in_specs=[pl.BlockSpec((1, rows, D), lambda b, t: (b, t, 0)),
                  _const_spec(norm_stack.shape),
                  _const_spec(lb_logits.shape),
                  _layer_spec(w_in_stack.shape, slot),
                  _const_spec(gnorm_stack.shape),
                  pl.BlockSpec((1, 1, xw, M), lambda b, t: (layer, b, 0, 0)),
                  pl.BlockSpec((1, 1, M, xw), lambda b, t: (layer, b, 0, 0)),
                  _layer_spec(w_out_stack.shape, slot)],
        out_specs=pl.BlockSpec((1, rows, D), lambda b, t: (b, t, 0)),
        scratch_shapes=[pltpu.VMEM((rows, D), BF16),
                        pltpu.VMEM((rows, D), F32),
                        pltpu.VMEM((rows, D), F32),
                        pltpu.VMEM((rows, D), F32),
                        pltpu.VMEM((rows, D), BF16),
                        pltpu.VMEM((rows, D), F32),
                        pltpu.VMEM((rows, xw), BF16),
                        pltpu.VMEM((rows, D), BF16),
                        pltpu.VMEM((rows, D), BF16),
                        pltpu.VMEM((rows, D), BF16),
                        pltpu.VMEM((n_ch, D), F32),
                        pltpu.VMEM((n_ch * HG_HEADS, HG_CHUNK, HG_CHUNK), BF16),
                        pltpu.VMEM((n_ch * HG_HEADS, hd, hd), F32),
                        pltpu.VMEM((n_ch * HG_HEADS, hd, hd), BF16),
                        pltpu.VMEM((rows, D), F32),
                        pltpu.VMEM((rows, D), BF16),
                        pltpu.VMEM((HG_HEADS, hd, hd), F32)],
        compiler_params=_params(2),
        name="hgrn_mix",
    )(x, norm_stack, lb_logits, w_in_stack, gnorm_stack, kt, mv, w_out_stack)


def _gmlp_kernel(x_ref, g_ref, win_ref, lng_ref, lnb_ref, ws_ref, bs_ref, kt_ref, mv_ref, wout_ref, o_ref,
                 h_ref, v_ref, acc_ref, *, layer, slot):
    rows, D = x_ref.shape
    W = lng_ref.shape[1]
    C = GM_CHUNK
    gd = W // GM_GROUPS
    n_chunks = rows // C

    h_ref[...] = _rms(x_ref[...], g_ref[layer:layer + 1, :]).astype(BF16)
    hb = h_ref[...]

    def w_in(col0):
        return win_ref[0, :, col0:col0 + gd].astype(BF16)

    def w_out(row0, n_rows):
        return wout_ref[0, row0:row0 + n_rows, :].astype(BF16)

    gv = [_gelu(_dot(hb, w_in(W + g * gd))) for g in range(GM_GROUPS)]
    mu = functools.reduce(lambda a, b: a + b, [jnp.sum(t, axis=-1, keepdims=True) for t in gv]) / W

    xa_width = win_ref.shape[2] - 2 * W
    xa_dim = xa_width // XA_HEADS
    qx = jnp.concatenate([_dot(hb, w_in(2 * W + j * gd)).astype(BF16) for j in range(xa_width // gd)], axis=1)
    omem = _attn_values(_attn_scores(qx, kt_ref), mv_ref)
    acc_ref[...] = x_ref[...]
    for h in range(XA_HEADS):
        acc_ref[...] += _dot(omem[h], w_out(W + h * xa_dim, xa_dim))

    xc = [t - mu for t in gv]
    var = functools.reduce(lambda a, b: a + b, [jnp.sum(t * t, axis=-1, keepdims=True) for t in xc]) / W
    rstd = lax.rsqrt(var + EPS)
    for g in range(GM_GROUPS):
        cols = slice(g * gd, (g + 1) * gd)
        v_ref[:, cols] = (xc[g] * rstd * lng_ref[slot:slot + 1, cols] + lnb_ref[slot:slot + 1, cols]).astype(BF16)

    ri = lax.broadcasted_iota(jnp.int32, (C, C), 0)
    ci = lax.broadcasted_iota(jnp.int32, (C, C), 1)
    causal = ci <= ri

    def project_u(g):
        return _gelu(_dot(h_ref[...], w_in(g * gd)))

    def mix_and_project_out(g, u):
        cols = slice(g * gd, (g + 1) * gd)
        wm = jnp.where(causal, ws_ref[0, g], 0.0).astype(BF16)
        bias = bs_ref[0, :, g:g + 1]
        mixed = jnp.concatenate([_dot(wm, v_ref[c * C:(c + 1) * C, cols]) + bias for c in range(n_chunks)], axis=0)
        acc_ref[...] += _dot((u * mixed).astype(BF16), w_out(g * gd, gd))

    u_next = project_u(0)
    for g in range(GM_GROUPS):
        u = u_next
        if g + 1 < GM_GROUPS:
            u_next = project_u(g + 1)
        mix_and_project_out(g, u)

    o_ref[...] = acc_ref[...]


def _gmlp_mix(x2d, seq_len, norm_stack, layer, slot, w_in_stack, ln_g, ln_b, w_s, b_s_t, kt, mv, w_out_stack):
    N, D = x2d.shape
    rows = GMLP_ROWS
    M = kt.shape[-1]
    xw = kt.shape[-2]
    assert seq_len % rows == 0 and rows % GM_CHUNK == 0
    per_seq = seq_len // rows
    return pl.pallas_call(
        functools.partial(_gmlp_kernel, layer=layer, slot=slot),
        out_shape=jax.ShapeDtypeStruct((N, D), F32),
        grid=(N // rows,),
        in_specs=[pl.BlockSpec((rows, D), lambda i: (i, 0)),
                  _const_spec(norm_stack.shape),
                  _layer_spec(w_in_stack.shape, slot),
                  _const_spec(ln_g.shape),
                  _const_spec(ln_b.shape),
                  _layer_spec(w_s.shape, slot),
                  _layer_spec(b_s_t.shape, slot),
                  pl.BlockSpec((1, 1, xw, M), lambda i: (layer, i // per_seq, 0, 0)),
                  pl.BlockSpec((1, 1, M, xw), lambda i: (layer, i // per_seq, 0, 0)),
                  _layer_spec(w_out_stack.shape, slot)],
        out_specs=pl.BlockSpec((rows, D), lambda i: (i, 0)),
        scratch_shapes=[pltpu.VMEM((rows, D), BF16),
                        pltpu.VMEM((rows, ln_g.shape[1]), BF16),
                        pltpu.VMEM((rows, D), F32)],
        compiler_params=_params(1),
        name="gmlp_mix",
    )(x2d, norm_stack, w_in_stack, ln_g, ln_b, w_s, b_s_t, kt, mv, w_out_stack)


def kernel(x, mem, mem_norm, lb_logits, ffn1_norm, ffn1_w_in, ffn1_w_out, mix_norm, mem_w_kv,
           hgrn_w_in, hgrn_gnorm, hgrn_w_out, gmlp_w_in, gmlp_ln_g, gmlp_ln_b, gmlp_w_s, gmlp_b_s,
           gmlp_w_out, ffn2_norm, ffn2_w_in, ffn2_w_out, final_norm):
    B, T, D = x.shape
    depth = ffn1_norm.shape[0]

    kt, mv = _mem_kv(mem, mem_norm, mem_w_kv)
    gmlp_b_s_t = jnp.swapaxes(gmlp_b_s, 1, 2)
    x = x.reshape(B * T, D)
    for i in range(depth):
        x = _ffn(x, ffn1_norm, ffn1_w_in, ffn1_w_out, i)
        j = i // 2
        if i % 2 == 0:
            x = _hgrn_mix(x.reshape(B, T, D), mix_norm, lb_logits, i, j, hgrn_w_in, hgrn_gnorm, kt, mv,
                          hgrn_w_out).reshape(B * T, D)
        else:
            x = _gmlp_mix(x, T, mix_norm, i, j, gmlp_w_in, gmlp_ln_g, gmlp_ln_b, gmlp_w_s, gmlp_b_s_t, kt, mv,
                          gmlp_w_out)
        x = _ffn(x, ffn2_norm, ffn2_w_in, ffn2_w_out, i, final_norm if i == depth - 1 else None)
    return x.reshape(B, T, D)
```

```python
import functools

import jax
import jax.numpy as jnp
from jax import lax
from jax.experimental import pallas as pl
from jax.experimental.pallas import tpu as pltpu

EPS = 1e-6
HG_HEADS = 8
HG_CHUNK = 64
GM_CHUNK = 128
GM_GROUPS = 8
XA_HEADS = 4

V7X_VMEM_BYTES = 64 * 1024 * 1024
VMEM_LIMIT_BYTES = V7X_VMEM_BYTES - 5 * 1024 * 1024
MXU_COLS = 256
FFN_ROWS = 1024
HGRN_ROWS = 512
GMLP_ROWS = 512
MEM_KV_BATCH = 4

F32 = jnp.float32
BF16 = jnp.bfloat16


def _dot(a, b):
    return jnp.dot(a, b, preferred_element_type=F32)


def _dot_nt(a, b):
    return lax.dot_general(a, b, (((1,), (1,)), ((), ())), preferred_element_type=F32)


def _dot_tn(a, b):
    return lax.dot_general(a, b, (((0,), (0,)), ((), ())), preferred_element_type=F32)


def _rms(x, g):
    return x * lax.rsqrt(jnp.mean(x * x, axis=-1, keepdims=True) + EPS) * g


LOG2E = 1.4426950408889634


def _sigmoid(x):
    return 1.0 / (1.0 + jnp.exp2(x * (-LOG2E)))


def _silu(x):
    return x * _sigmoid(x)


def _gelu(x):
    return 0.5 * x * (1.0 + lax.erf(x * 0.7071067811865476))


def _const_spec(shape):
    nd = len(shape)
    return pl.BlockSpec(shape, lambda *_: (0,) * nd, pipeline_mode=pl.Buffered(1))


def _layer_spec(stack_shape, layer):
    nd = len(stack_shape)
    return pl.BlockSpec((1,) + tuple(stack_shape[1:]), lambda *_: (layer,) + (0,) * (nd - 1),
                        pipeline_mode=pl.Buffered(1))


def _params(n_grid):
    return pltpu.CompilerParams(
        dimension_semantics=("arbitrary",) * n_grid,
        vmem_limit_bytes=VMEM_LIMIT_BYTES,
    )


def _mem_kv_kernel(mem_ref, g_ref, wkv_ref, kt_ref, v_ref, *, scale):
    nb, M, D = mem_ref.shape
    width = v_ref.shape[-1]
    mn = _rms(mem_ref[...].reshape(nb * M, D), g_ref[...]).astype(BF16)
    kv = _dot(mn, wkv_ref[0].astype(BF16))
    for j in range(nb):
        rs = slice(j * M, (j + 1) * M)
        kt_ref[0, j] = (kv[rs, :width] * scale).T.astype(BF16)
        v_ref[0, j] = kv[rs, width:].astype(BF16)


def _mem_kv(mem, mem_norm, w_kv):
    B, M, D = mem.shape
    depth, _, two_w = w_kv.shape
    width = two_w // 2
    scale = float((width // XA_HEADS) ** -0.5)
    nb = MEM_KV_BATCH
    assert B % nb == 0
    return pl.pallas_call(
        functools.partial(_mem_kv_kernel, scale=scale),
        out_shape=(jax.ShapeDtypeStruct((depth, B, width, M), BF16),
                   jax.ShapeDtypeStruct((depth, B, M, width), BF16)),
        grid=(depth, B // nb),
        in_specs=[pl.BlockSpec((nb, M, D), lambda l, b: (b, 0, 0)),
                  pl.BlockSpec((1, D), lambda l, b: (0, 0)),
                  pl.BlockSpec((1, D, two_w), lambda l, b: (l, 0, 0))],
        out_specs=(pl.BlockSpec((1, nb, width, M), lambda l, b: (l, b, 0, 0)),
                   pl.BlockSpec((1, nb, M, width), lambda l, b: (l, b, 0, 0))),
        compiler_params=_params(2),
        name="mem_kv",
    )(mem, mem_norm.reshape(1, D), w_kv)


def _ffn_kernel(x_ref, g_ref, win_hbm, wout_hbm, fg_ref, o_ref,
                xn_ref, win_ref, wout_ref, sg_ref, su_ref, so_ref, sem, *, layer, final_norm):
    d_ff = wout_ref.shape[0]
    n_chunks = d_ff // MXU_COLS

    def slab_copies(c, slot):
        lo = c * MXU_COLS
        return (
            pltpu.make_async_copy(win_hbm.at[layer, :, pl.ds(lo, MXU_COLS)], sg_ref.at[slot], sem.at[0, slot]),
            pltpu.make_async_copy(win_hbm.at[layer, :, pl.ds(d_ff + lo, MXU_COLS)], su_ref.at[slot], sem.at[1, slot]),
            pltpu.make_async_copy(wout_hbm.at[layer, pl.ds(lo, MXU_COLS), :], so_ref.at[slot], sem.at[2, slot]),
        )

    def fetch(c):
        for cp in slab_copies(c, c % 2):
            cp.start()

    def land(c):
        slot = c % 2
        lo = c * MXU_COLS
        for cp in slab_copies(c, slot):
            cp.wait()
        win_ref[:, lo:lo + MXU_COLS] = sg_ref[slot].astype(BF16)
        win_ref[:, d_ff + lo:d_ff + lo + MXU_COLS] = su_ref[slot].astype(BF16)
        wout_ref[lo:lo + MXU_COLS, :] = so_ref[slot].astype(BF16)

    def normalise_input():
        xn_ref[...] = _rms(x_ref[...], g_ref[layer:layer + 1, :]).astype(BF16)

    def compute(c):
        lo = c * MXU_COLS
        xn = xn_ref[...]
        gate = _dot(xn, win_ref[:, lo:lo + MXU_COLS])
        up = _dot(xn, win_ref[:, d_ff + lo:d_ff + lo + MXU_COLS])
        act = (gate * _sigmoid(gate) * (0.5 * up)).astype(BF16)
        part = _dot(act, wout_ref[lo:lo + MXU_COLS, :])
        if c == 0:
            o_ref[...] = x_ref[...] + part
        else:
            o_ref[...] += part

    first = pl.program_id(0) == 0

    @pl.when(first)
    def _():
        fetch(0)
        fetch(1)
        normalise_input()
        for c in range(n_chunks):
            land(c)
            if c + 2 < n_chunks:
                fetch(c + 2)
            compute(c)

    @pl.when(jnp.logical_not(first))
    def _():
        normalise_input()
        for c in range(n_chunks):
            compute(c)

    if final_norm:
        o_ref[...] = _rms(o_ref[...], fg_ref[...])


def _ffn(x2d, norm_stack, w_in_stack, w_out_stack, layer, final_g=None):
    N, D = x2d.shape
    d_ff = w_out_stack.shape[1]
    assert d_ff % MXU_COLS == 0 and N % FFN_ROWS == 0
    final_norm = final_g is not None
    fg = final_g.reshape(1, D) if final_norm else norm_stack[:1]
    return pl.pallas_call(
        functools.partial(_ffn_kernel, layer=layer, final_norm=final_norm),
        out_shape=jax.ShapeDtypeStruct((N, D), F32),
        grid=(N // FFN_ROWS,),
        in_specs=[pl.BlockSpec((FFN_ROWS, D), lambda i: (i, 0)),
                  _const_spec(norm_stack.shape),
                  pl.BlockSpec(memory_space=pl.ANY),
                  pl.BlockSpec(memory_space=pl.ANY),
                  _const_spec((1, D))],
        out_specs=pl.BlockSpec((FFN_ROWS, D), lambda i: (i, 0)),
        scratch_shapes=[pltpu.VMEM((FFN_ROWS, D), BF16),
                        pltpu.VMEM((D, 2 * d_ff), BF16),
                        pltpu.VMEM((d_ff, D), BF16),
                        pltpu.VMEM((2, D, MXU_COLS), F32),
                        pltpu.VMEM((2, D, MXU_COLS), F32),
                        pltpu.VMEM((2, MXU_COLS, D), F32),
                        pltpu.SemaphoreType.DMA((3, 2))],
        compiler_params=_params(1),
        name="ffn_final" if final_norm else "ffn",
    )(x2d, norm_stack, w_in_stack, w_out_stack, fg)


def _attn_scores(qx, kt_ref):
    xa_dim = qx.shape[1] // XA_HEADS
    probs = []
    for h in range(XA_HEADS):
        cols = slice(h * xa_dim, (h + 1) * xa_dim)
        s = _dot(qx[:, cols], kt_ref[0, 0, cols, :])
        p = jnp.exp(s - jnp.max(s, axis=-1, keepdims=True))
        probs.append((p.astype(BF16), jnp.sum(p, axis=-1, keepdims=True)))
    return probs


def _attn_values(probs, mv_ref):
    xa_dim = mv_ref.shape[-1] // XA_HEADS
    outs = []
    for h, (p, denom) in enumerate(probs):
        o = _dot(p, mv_ref[0, 0, :, h * xa_dim:(h + 1) * xa_dim]) / denom
        outs.append(o.astype(BF16))
    return outs


def _hgrn_kernel(x_ref, g_ref, lbl_ref, win_ref, gn_ref, kt_ref, mv_ref, wout_ref, o_ref,
                 h_ref, q_ref, k_ref, b_ref, v_ref, sg_ref, qx_ref, qd_ref, ki_ref, kr_ref, dec_ref,
                 a_ref, ut_ref, sp_ref, oh_ref, on_ref, st_ref, *, layer, slot):
    rows, D = h_ref.shape
    hd = D // HG_HEADS
    C = HG_CHUNK
    n_chunks = rows // C
    PW = MXU_COLS
    n_slabs = D // PW
    xa_dim = (win_ref.shape[2] - 4 * D) // XA_HEADS

    @pl.when(pl.program_id(1) == 0)
    def _():
        st_ref[...] = jnp.zeros_like(st_ref)

    half = rows // 2
    for r in range(0, rows, half):
        h_ref[r:r + half, :] = _rms(x_ref[0, r:r + half, :], g_ref[layer:layer + 1, :]).astype(BF16)
    hb = h_ref[...]

    def w_in(col0):
        return win_ref[0, :, col0:col0 + PW].astype(BF16)

    def w_out(row0, n_rows):
        return wout_ref[0, row0:row0 + n_rows, :].astype(BF16)

    n_lb = lbl_ref.shape[0]
    lrows = [lbl_ref[r:r + 1, :] for r in range(n_lb)]
    lmax = functools.reduce(jnp.maximum, lrows)
    lexp = [jnp.exp(r - lmax) for r in lrows]
    lb = functools.reduce(lambda a, b: a + b, lexp[:layer + 1]) / functools.reduce(lambda a, b: a + b, lexp)

    R = min(rows, MXU_COLS)
    ri = lax.broadcasted_iota(jnp.int32, (R, R), 0)
    ci = lax.broadcasted_iota(jnp.int32, (R, R), 1)
    tril = jnp.where((ci <= ri) & (ci >= (ri // C) * C), 1.0, 0.0).astype(BF16)

    def forget_gate(p):
        cols = slice(p * PW, (p + 1) * PW)
        if p == 0:
            wf = w_in(D)
            zf = jnp.concatenate([_dot(h_ref[r:r + half, :], wf) for r in range(0, rows, half)], axis=0)
        else:
            zf = _dot(hb, w_in(D + p * PW))
        f = lb[:, cols] + (1.0 - lb[:, cols]) * _sigmoid(zf)
        k_ref[:, cols] = 1.0 - f
        lf = jnp.log(f)
        lf_hi = lf.astype(BF16)
        rem = lf - lf_hi.astype(F32)
        lf_mid = rem.astype(BF16)
        lf_lo = (rem - lf_mid.astype(F32)).astype(BF16)
        return lf_hi, lf_mid, lf_lo

    split = forget_gate(0)
    for p in range(n_slabs):
        nxt = forget_gate(p + 1) if p + 1 < n_slabs else None
        q_ref[:, p * PW:(p + 1) * PW] = _silu(_dot(hb, w_in(p * PW)))
        for r in range(rows // R):
            rr = slice(r * R, (r + 1) * R)
            b_ref[rr, p * PW:(p + 1) * PW] = (_dot(tril, split[0][rr]) + _dot(tril, split[1][rr])
                                              + _dot(tril, split[2][rr])) * LOG2E
        split = nxt

    for p in range(n_slabs):
        v_ref[:, p * PW:(p + 1) * PW] = _dot(hb, w_in(2 * D + p * PW)).astype(BF16)

    cr = lax.broadcasted_iota(jnp.int32, (C, C), 0)
    cc = lax.broadcasted_iota(jnp.int32, (C, C), 1)
    causal = cc <= cr
    gn = gn_ref[slot:slot + 1, :]

    def decay(c):
        rs = slice(c * C, (c + 1) * C)
        b2 = b_ref[rs, :]
        bl2 = b_ref[(c + 1) * C - 1:(c + 1) * C, :]
        kk = k_ref[rs, :]
        qd_ref[rs, :] = (q_ref[rs, :] * jnp.exp2(b2)).astype(BF16)
        ki_ref[rs, :] = (kk * jnp.exp2(-b2)).astype(BF16)
        kr_ref[rs, :] = (kk * jnp.exp2(bl2 - b2)).astype(BF16)
        dec_ref[c:c + 1, :] = jnp.exp2(bl2)

    def side_projection(i):
        p = i % n_slabs
        cols = slice(p * PW, (p + 1) * PW)
        if i < n_slabs:
            sg_ref[:, cols] = _silu(_dot(hb, w_in(3 * D + p * PW)))
        else:
            qx_ref[:, cols] = _dot(hb, w_in(4 * D + p * PW)).astype(BF16)

    def chunk_scores(c):
        rs = slice(c * C, (c + 1) * C)
        for hh in range(HG_HEADS):
            cols = slice(hh * hd, (hh + 1) * hd)
            a = _dot_nt(qd_ref[rs, cols], ki_ref[rs, cols])
            a_ref[c * HG_HEADS + hh] = jnp.where(causal, a, 0.0).astype(BF16)
            ut_ref[c * HG_HEADS + hh] = _dot_tn(v_ref[rs, cols], kr_ref[rs, cols])

    n_side = 2 * n_slabs
    per_chunk = -(-n_side // n_chunks)
    decay(0)
    for c in range(n_chunks):
        for i in range(c * per_chunk, min((c + 1) * per_chunk, n_side)):
            side_projection(i)
        if c + 1 < n_chunks:
            decay(c + 1)
        chunk_scores(c)

    probs = _attn_scores(qx_ref[...], kt_ref)

    def scan(c):
        for hh in range(HG_HEADS):
            cols = slice(hh * hd, (hh + 1) * hd)
            st = st_ref[hh]
            sp_ref[c * HG_HEADS + hh] = st.astype(BF16).T
            st_ref[hh] = st * dec_ref[c:c + 1, cols] + ut_ref[c * HG_HEADS + hh]

    def chunk_norm(c):
        rs = slice(c * C, (c + 1) * C)
        for hh in range(HG_HEADS):
            cols = slice(hh * hd, (hh + 1) * hd)
            o = oh_ref[rs, cols]
            o = o * lax.rsqrt(jnp.mean(o * o, axis=-1, keepdims=True) + EPS) * gn
            on_ref[rs, cols] = (o * sg_ref[rs, cols]).astype(BF16)

    mem_out = [None] * XA_HEADS

    def mem_values(h):
        def run():
            p, denom = probs[h]
            mem_out[h] = (_dot(p, mv_ref[0, 0, :, h * xa_dim:(h + 1) * xa_dim]) / denom).astype(BF16)
        return run

    def mem_project(h, n):
        def run():
            cs = slice(n * PW, (n + 1) * PW)
            part = _dot(mem_out[h], wout_ref[0, D + h * xa_dim:D + (h + 1) * xa_dim, n * PW:(n + 1) * PW].astype(BF16))
            if h == 0:
                o_ref[0, :, cs] = x_ref[0, :, cs] + part
            else:
                o_ref[0, :, cs] += part
        return run

    fillers = []
    for h in range(XA_HEADS):
        fillers.append(mem_values(h))
        fillers.extend(mem_project(h, n) for n in range(n_slabs))
    slots = n_chunks * (HG_HEADS // 2)
    per_slot = -(-len(fillers) // slots)

    def chunk_outputs(c):
        rs = slice(c * C, (c + 1) * C)
        for hh in range(HG_HEADS):
            cols = slice(hh * hd, (hh + 1) * hd)
            lhs = jnp.concatenate([qd_ref[rs, cols], a_ref[c * HG_HEADS + hh]], axis=1)
            rhs = jnp.concatenate([sp_ref[c * HG_HEADS + hh], v_ref[rs, cols]], axis=0)
            oh_ref[rs, cols] = _dot(lhs, rhs)
            if hh % 2 == 1:
                for _ in range(per_slot):
                    if fillers:
                        fillers.pop(0)()

    for c in range(n_chunks):
        scan(c)
        chunk_outputs(c)
        if c >= 1:
            chunk_norm(c - 1)
    while fillers:
        fillers.pop(0)()
    chunk_norm(n_chunks - 1)

    for p in range(n_slabs):
        o_ref[0] += _dot(on_ref[:, p * PW:(p + 1) * PW], w_out(p * PW, PW))


def _hgrn_mix(x, norm_stack, lb_logits, layer, slot, w_in_stack, gnorm_stack, kt, mv, w_out_stack):
    B, T, D = x.shape
    rows = HGRN_ROWS
    hd = D // HG_HEADS
    M = kt.shape[-1]
    xw = kt.shape[-2]
    assert T % rows == 0 and rows % HG_CHUNK == 0 and D % MXU_COLS == 0
    n_ch = rows // HG_CHUNK
    return pl.pallas_call(
        functools.partial(_hgrn_kernel, layer=layer, slot=slot),
        out_shape=jax.ShapeDtypeStruct((B, T, D), F32),
        grid=(B, T // rows),
        in_specs=[pl.BlockSpec((1, rows, D), lambda b, t: (b, t, 0)),
                  _const_spec(norm_stack.shape),
                  _const_spec(lb_logits.shape),
                  _layer_spec(w_in_stack.shape, slot),
                  _const_spec(gnorm_stack.shape),
                  pl.BlockSpec((1, 1, xw, M), lambda b, t: (layer, b, 0, 0)),
                  pl.BlockSpec((1, 1, M, xw), lambda b, t: (layer, b, 0, 0)),
                  _layer_spec(w_out_stack.shape, slot)],
        out_specs=pl.BlockSpec((1, rows, D), lambda b, t: (b, t, 0)),
        scratch_shapes=[pltpu.VMEM((rows, D), BF16),
                        pltpu.VMEM((rows, D), F32),
                        pltpu.VMEM((rows, D), F32),
                        pltpu.VMEM((rows, D), F32),
                        pltpu.VMEM((rows, D), BF16),
                        pltpu.VMEM((rows, D), F32),
                        pltpu.VMEM((rows, xw), BF16),
                        pltpu.VMEM((rows, D), BF16),
                        pltpu.VMEM((rows, D), BF16),
                        pltpu.VMEM((rows, D), BF16),
                        pltpu.VMEM((n_ch, D), F32),
                        pltpu.VMEM((n_ch * HG_HEADS, HG_CHUNK, HG_CHUNK), BF16),
                        pltpu.VMEM((n_ch * HG_HEADS, hd, hd), F32),
                        pltpu.VMEM((n_ch * HG_HEADS, hd, hd), BF16),
                        pltpu.VMEM((rows, D), F32),
                        pltpu.VMEM((rows, D), BF16),
                        pltpu.VMEM((HG_HEADS, hd, hd), F32)],
        compiler_params=_params(2),
        name="hgrn_mix",
    )(x, norm_stack, lb_logits, w_in_stack, gnorm_stack, kt, mv, w_out_stack)


def _gmlp_kernel(x_ref, g_ref, win_ref, lng_ref, lnb_ref, ws_ref, bs_ref, kt_ref, mv_ref, wout_ref, o_ref,
                 h_ref, v_ref, acc_ref, *, layer, slot):
    rows, D = x_ref.shape
    W = lng_ref.shape[1]
    C = GM_CHUNK
    gd = W // GM_GROUPS
    n_chunks = rows // C

    h_ref[...] = _rms(x_ref[...], g_ref[layer:layer + 1, :]).astype(BF16)
    hb = h_ref[...]

    def w_in(col0):
        return win_ref[0, :, col0:col0 + gd].astype(BF16)

    def w_out(row0, n_rows):
        return wout_ref[0, row0:row0 + n_rows, :].astype(BF16)

    gv = [_gelu(_dot(hb, w_in(W + g * gd))) for g in range(GM_GROUPS)]
    mu = functools.reduce(lambda a, b: a + b, [jnp.sum(t, axis=-1, keepdims=True) for t in gv]) / W

    xa_width = win_ref.shape[2] - 2 * W
    xa_dim = xa_width // XA_HEADS
    qx = jnp.concatenate([_dot(hb, w_in(2 * W + j * gd)).astype(BF16) for j in range(xa_width // gd)], axis=1)
    omem = _attn_values(_attn_scores(qx, kt_ref), mv_ref)
    acc_ref[...] = x_ref[...]
    for h in range(XA_HEADS):
        acc_ref[...] += _dot(omem[h], w_out(W + h * xa_dim, xa_dim))

    xc = [t - mu for t in gv]
    var = functools.reduce(lambda a, b: a + b, [jnp.sum(t * t, axis=-1, keepdims=True) for t in xc]) / W
    rstd = lax.rsqrt(var + EPS)
    for g in range(GM_GROUPS):
        cols = slice(g * gd, (g + 1) * gd)
        v_ref[:, cols] = (xc[g] * rstd * lng_ref[slot:slot + 1, cols] + lnb_ref[slot:slot + 1, cols]).astype(BF16)

    ri = lax.broadcasted_iota(jnp.int32, (C, C), 0)
    ci = lax.broadcasted_iota(jnp.int32, (C, C), 1)
    causal = ci <= ri

    def project_u(g):
        return _gelu(_dot(h_ref[...], w_in(g * gd)))

    def mix_and_project_out(g, u):
        cols = slice(g * gd, (g + 1) * gd)
        wm = jnp.where(causal, ws_ref[0, g], 0.0).astype(BF16)
        bias = bs_ref[0, :, g:g + 1]
        mixed = jnp.concatenate([_dot(wm, v_ref[c * C:(c + 1) * C, cols]) + bias for c in range(n_chunks)], axis=0)
        acc_ref[...] += _dot((u * mixed).astype(BF16), w_out(g * gd, gd))

    u_next = project_u(0)
    for g in range(GM_GROUPS):
        u = u_next
        if g + 1 < GM_GROUPS:
            u_next = project_u(g + 1)
        mix_and_project_out(g, u)

    o_ref[...] = acc_ref[...]


def _gmlp_mix(x2d, seq_len, norm_stack, layer, slot, w_in_stack, ln_g, ln_b, w_s, b_s_t, kt, mv, w_out_stack):
    N, D = x2d.shape
    rows = GMLP_ROWS
    M = kt.shape[-1]
    xw = kt.shape[-2]
    assert seq_len % rows == 0 and rows % GM_CHUNK == 0
    per_seq = seq_len // rows
    return pl.pallas_call(
        functools.partial(_gmlp_kernel, layer=layer, slot=slot),
        out_shape=jax.ShapeDtypeStruct((N, D), F32),
        grid=(N // rows,),
        in_specs=[pl.BlockSpec((rows, D), lambda i: (i, 0)),
                  _const_spec(norm_stack.shape),
                  _layer_spec(w_in_stack.shape, slot),
                  _const_spec(ln_g.shape),
                  _const_spec(ln_b.shape),
                  _layer_spec(w_s.shape, slot),
                  _layer_spec(b_s_t.shape, slot),
                  pl.BlockSpec((1, 1, xw, M), lambda i: (layer, i // per_seq, 0, 0)),
                  pl.BlockSpec((1, 1, M, xw), lambda i: (layer, i // per_seq, 0, 0)),
                  _layer_spec(w_out_stack.shape, slot)],
        out_specs=pl.BlockSpec((rows, D), lambda i: (i, 0)),
        scratch_shapes=[pltpu.VMEM((rows, D), BF16),
                        pltpu.VMEM((rows, ln_g.shape[1]), BF16),
                        pltpu.VMEM((rows, D), F32)],
        compiler_params=_params(1),
        name="gmlp_mix",
    )(x2d, norm_stack, w_in_stack, ln_g, ln_b, w_s, b_s_t, kt, mv, w_out_stack)


def kernel(x, mem, mem_norm, lb_logits, ffn1_norm, ffn1_w_in, ffn1_w_out, mix_norm, mem_w_kv,
           hgrn_w_in, hgrn_gnorm, hgrn_w_out, gmlp_w_in, gmlp_ln_g, gmlp_ln_b, gmlp_w_s, gmlp_b_s,
           gmlp_w_out, ffn2_norm, ffn2_w_in, ffn2_w_out, final_norm):
    B, T, D = x.shape
    depth = ffn1_norm.shape[0]

    kt, mv = _mem_kv(mem, mem_norm, mem_w_kv)
    gmlp_b_s_t = jnp.swapaxes(gmlp_b_s, 1, 2)
    x = x.reshape(B * T, D)
    for i in range(depth):
        x = _ffn(x, ffn1_norm, ffn1_w_in, ffn1_w_out, i)
        j = i // 2
        if i % 2 == 0:
            x = _hgrn_mix(x.reshape(B, T, D), mix_norm, lb_logits, i, j, hgrn_w_in, hgrn_gnorm, kt, mv,
                          hgrn_w_out).reshape(B * T, D)
        else:
            x = _gmlp_mix(x, T, mix_norm, i, j, gmlp_w_in, gmlp_ln_g, gmlp_ln_b, gmlp_w_s, gmlp_b_s_t, kt, mv,
                          gmlp_w_out)
        x = _ffn(x, ffn2_norm, ffn2_w_in, ffn2_w_out, i, final_norm if i == depth - 1 else None)
    return x.reshape(B, T, D)
```

```python
import functools

import jax
import jax.numpy as jnp
from jax import lax
from jax.experimental import pallas as pl
from jax.experimental.pallas import tpu as pltpu

EPS = 1e-6
HG_HEADS = 8
HG_CHUNK = 64
GM_CHUNK = 128
GM_GROUPS = 8
XA_HEADS = 4

V7X_VMEM_BYTES = 64 * 1024 * 1024
VMEM_LIMIT_BYTES = V7X_VMEM_BYTES - 5 * 1024 * 1024
MXU_COLS = 256
FFN_ROWS = 1024
FFN_STAGES = 4
HGRN_ROWS = 512
GMLP_ROWS = 512
MEM_KV_BATCH = 4

F32 = jnp.float32
BF16 = jnp.bfloat16


def _dot(a, b):
    return jnp.dot(a, b, preferred_element_type=F32)


def _dot_nt(a, b):
    return lax.dot_general(a, b, (((1,), (1,)), ((), ())), preferred_element_type=F32)


def _dot_tn(a, b):
    return lax.dot_general(a, b, (((0,), (0,)), ((), ())), preferred_element_type=F32)


def _rms(x, g):
    return x * lax.rsqrt(jnp.mean(x * x, axis=-1, keepdims=True) + EPS) * g


LOG2E = 1.4426950408889634


def _sigmoid(x):
    return 1.0 / (1.0 + jnp.exp2(x * (-LOG2E)))


def _silu(x):
    return x * _sigmoid(x)


def _gelu(x):
    return 0.5 * x * (1.0 + lax.erf(x * 0.7071067811865476))


def _const_spec(shape):
    nd = len(shape)
    return pl.BlockSpec(shape, lambda *_: (0,) * nd, pipeline_mode=pl.Buffered(1))


def _layer_spec(stack_shape, layer):
    nd = len(stack_shape)
    return pl.BlockSpec((1,) + tuple(stack_shape[1:]), lambda *_: (layer,) + (0,) * (nd - 1),
                        pipeline_mode=pl.Buffered(1))


def _params(n_grid):
    return pltpu.CompilerParams(
        dimension_semantics=("arbitrary",) * n_grid,
        vmem_limit_bytes=VMEM_LIMIT_BYTES,
    )


def _mem_kv_kernel(mem_ref, g_ref, wkv_ref, kt_ref, v_ref, *, scale):
    nb, M, D = mem_ref.shape
    width = v_ref.shape[-1]
    mn = _rms(mem_ref[...].reshape(nb * M, D), g_ref[...]).astype(BF16)
    kv = _dot(mn, wkv_ref[0].astype(BF16))
    for j in range(nb):
        rs = slice(j * M, (j + 1) * M)
        kt_ref[0, j] = (kv[rs, :width] * scale).T.astype(BF16)
        v_ref[0, j] = kv[rs, width:].astype(BF16)


def _mem_kv(mem, mem_norm, w_kv):
    B, M, D = mem.shape
    depth, _, two_w = w_kv.shape
    width = two_w // 2
    scale = float((width // XA_HEADS) ** -0.5)
    nb = MEM_KV_BATCH
    assert B % nb == 0
    return pl.pallas_call(
        functools.partial(_mem_kv_kernel, scale=scale),
        out_shape=(jax.ShapeDtypeStruct((depth, B, width, M), BF16),
                   jax.ShapeDtypeStruct((depth, B, M, width), BF16)),
        grid=(depth, B // nb),
        in_specs=[pl.BlockSpec((nb, M, D), lambda l, b: (b, 0, 0)),
                  pl.BlockSpec((1, D), lambda l, b: (0, 0)),
                  pl.BlockSpec((1, D, two_w), lambda l, b: (l, 0, 0))],
        out_specs=(pl.BlockSpec((1, nb, width, M), lambda l, b: (l, b, 0, 0)),
                   pl.BlockSpec((1, nb, M, width), lambda l, b: (l, b, 0, 0))),
        compiler_params=_params(2),
        name="mem_kv",
    )(mem, mem_norm.reshape(1, D), w_kv)


def _ffn_kernel(x_ref, g_ref, win_hbm, wout_hbm, fg_ref, o_ref,
                xn_ref, win_ref, wout_ref, sg_ref, su_ref, so_ref, sem, *, layer, final_norm):
    d_ff = wout_ref.shape[0]
    n_chunks = d_ff // MXU_COLS

    def slab_copies(c, slot):
        lo = c * MXU_COLS
        return (
            pltpu.make_async_copy(win_hbm.at[layer, :, pl.ds(lo, MXU_COLS)], sg_ref.at[slot], sem.at[0, slot]),
            pltpu.make_async_copy(win_hbm.at[layer, :, pl.ds(d_ff + lo, MXU_COLS)], su_ref.at[slot], sem.at[1, slot]),
            pltpu.make_async_copy(wout_hbm.at[layer, pl.ds(lo, MXU_COLS), :], so_ref.at[slot], sem.at[2, slot]),
        )

    def fetch(c):
        for cp in slab_copies(c, c % FFN_STAGES):
            cp.start()

    def land(c):
        slot = c % FFN_STAGES
        lo = c * MXU_COLS
        for cp in slab_copies(c, slot):
            cp.wait()
        win_ref[:, lo:lo + MXU_COLS] = sg_ref[slot].astype(BF16)
        win_ref[:, d_ff + lo:d_ff + lo + MXU_COLS] = su_ref[slot].astype(BF16)
        wout_ref[lo:lo + MXU_COLS, :] = so_ref[slot].astype(BF16)

    def normalise_input():
        xn_ref[...] = _rms(x_ref[...], g_ref[layer:layer + 1, :]).astype(BF16)

    def compute(c):
        lo = c * MXU_COLS
        xn = xn_ref[...]
        gate = _dot(xn, win_ref[:, lo:lo + MXU_COLS])
        up = _dot(xn, win_ref[:, d_ff + lo:d_ff + lo + MXU_COLS])
        act = (gate * _sigmoid(gate) * (0.5 * up)).astype(BF16)
        part = _dot(act, wout_ref[lo:lo + MXU_COLS, :])
        if c == 0:
            o_ref[...] = x_ref[...] + part
        else:
            o_ref[...] += part

    first = pl.program_id(0) == 0

    @pl.when(first)
    def _():
        for c in range(min(FFN_STAGES, n_chunks)):
            fetch(c)
        normalise_input()
        for c in range(n_chunks):
            land(c)
            if c + FFN_STAGES < n_chunks:
                fetch(c + FFN_STAGES)
            compute(c)

    @pl.when(jnp.logical_not(first))
    def _():
        normalise_input()
        for c in range(n_chunks):
            compute(c)

    if final_norm:
        o_ref[...] = _rms(o_ref[...], fg_ref[...])


def _ffn(x2d, norm_stack, w_in_stack, w_out_stack, layer, final_g=None):
    N, D = x2d.shape
    d_ff = w_out_stack.shape[1]
    assert d_ff % MXU_COLS == 0 and N % FFN_ROWS == 0
    final_norm = final_g is not None
    fg = final_g.reshape(1, D) if final_norm else norm_stack[:1]
    return pl.pallas_call(
        functools.partial(_ffn_kernel, layer=layer, final_norm=final_norm),
        out_shape=jax.ShapeDtypeStruct((N, D), F32),
        grid=(N // FFN_ROWS,),
        in_specs=[pl.BlockSpec((FFN_ROWS, D), lambda i: (i, 0)),
                  _const_spec(norm_stack.shape),
                  pl.BlockSpec(memory_space=pl.ANY),
                  pl.BlockSpec(memory_space=pl.ANY),
                  _const_spec((1, D))],
        out_specs=pl.BlockSpec((FFN_ROWS, D), lambda i: (i, 0)),
        scratch_shapes=[pltpu.VMEM((FFN_ROWS, D), BF16),
                        pltpu.VMEM((D, 2 * d_ff), BF16),
                        pltpu.VMEM((d_ff, D), BF16),
                        pltpu.VMEM((FFN_STAGES, D, MXU_COLS), F32),
                        pltpu.VMEM((FFN_STAGES, D, MXU_COLS), F32),
                        pltpu.VMEM((FFN_STAGES, MXU_COLS, D), F32),
                        pltpu.SemaphoreType.DMA((3, FFN_STAGES))],
        compiler_params=_params(1),
        name="ffn_final" if final_norm else "ffn",
    )(x2d, norm_stack, w_in_stack, w_out_stack, fg)


def _attn_scores(qx, kt_ref):
    xa_dim = qx.shape[1] // XA_HEADS
    probs = []
    for h in range(XA_HEADS):
        cols = slice(h * xa_dim, (h + 1) * xa_dim)
        s = _dot(qx[:, cols], kt_ref[0, 0, cols, :])
        p = jnp.exp(s - jnp.max(s, axis=-1, keepdims=True))
        probs.append((p.astype(BF16), jnp.sum(p, axis=-1, keepdims=True)))
    return probs


def _attn_values(probs, mv_ref):
    xa_dim = mv_ref.shape[-1] // XA_HEADS
    outs = []
    for h, (p, denom) in enumerate(probs):
        o = _dot(p, mv_ref[0, 0, :, h * xa_dim:(h + 1) * xa_dim]) / denom
        outs.append(o.astype(BF16))
    return outs


def _hgrn_kernel(x_ref, g_ref, lbl_ref, win_ref, gn_ref, kt_ref, mv_ref, wout_ref, o_ref,
                 h_ref, q_ref, k_ref, b_ref, v_ref, sg_ref, qx_ref, qd_ref, ki_ref, kr_ref, dec_ref,
                 a_ref, ut_ref, sp_ref, oh_ref, on_ref, st_ref, *, layer, slot):
    rows, D = h_ref.shape
    hd = D // HG_HEADS
    C = HG_CHUNK
    n_chunks = rows // C
    PW = MXU_COLS
    n_slabs = D // PW
    xa_dim = (win_ref.shape[2] - 4 * D) // XA_HEADS

    @pl.when(pl.program_id(1) == 0)
    def _():
        st_ref[...] = jnp.zeros_like(st_ref)

    half = rows // 2
    for r in range(0, rows, half):
        h_ref[r:r + half, :] = _rms(x_ref[0, r:r + half, :], g_ref[layer:layer + 1, :]).astype(BF16)
    hb = h_ref[...]

    def w_in(col0):
        return win_ref[0, :, col0:col0 + PW].astype(BF16)

    def w_out(row0, n_rows):
        return wout_ref[0, row0:row0 + n_rows, :].astype(BF16)

    n_lb = lbl_ref.shape[0]
    lrows = [lbl_ref[r:r + 1, :] for r in range(n_lb)]
    lmax = functools.reduce(jnp.maximum, lrows)
    lexp = [jnp.exp(r - lmax) for r in lrows]
    lb = functools.reduce(lambda a, b: a + b, lexp[:layer + 1]) / functools.reduce(lambda a, b: a + b, lexp)

    R = min(rows, MXU_COLS)
    ri = lax.broadcasted_iota(jnp.int32, (R, R), 0)
    ci = lax.broadcasted_iota(jnp.int32, (R, R), 1)
    tril = jnp.where((ci <= ri) & (ci >= (ri // C) * C), 1.0, 0.0).astype(BF16)

    def forget_gate(p):
        cols = slice(p * PW, (p + 1) * PW)
        if p == 0:
            wf = w_in(D)
            zf = jnp.concatenate([_dot(h_ref[r:r + half, :], wf) for r in range(0, rows, half)], axis=0)
        else:
            zf = _dot(hb, w_in(D + p * PW))
        f = lb[:, cols] + (1.0 - lb[:, cols]) * _sigmoid(zf)
        k_ref[:, cols] = 1.0 - f
        lf = jnp.log(f)
        lf_hi = lf.astype(BF16)
        rem = lf - lf_hi.astype(F32)
        lf_mid = rem.astype(BF16)
        lf_lo = (rem - lf_mid.astype(F32)).astype(BF16)
        return lf_hi, lf_mid, lf_lo

    split = forget_gate(0)
    for p in range(n_slabs):
        nxt = forget_gate(p + 1) if p + 1 < n_slabs else None
        q_ref[:, p * PW:(p + 1) * PW] = _silu(_dot(hb, w_in(p * PW)))
        for r in range(rows // R):
            rr = slice(r * R, (r + 1) * R)
            b_ref[rr, p * PW:(p + 1) * PW] = (_dot(tril, split[0][rr]) + _dot(tril, split[1][rr])
                                              + _dot(tril, split[2][rr])) * LOG2E
        split = nxt

    for p in range(n_slabs):
        v_ref[:, p * PW:(p + 1) * PW] = _dot(hb, w_in(2 * D + p * PW)).astype(BF16)

    cr = lax.broadcasted_iota(jnp.int32, (C, C), 0)
    cc = lax.broadcasted_iota(jnp.int32, (C, C), 1)
    causal = cc <= cr
    gn = gn_ref[slot:slot + 1, :]

    def decay(c):
        rs = slice(c * C, (c + 1) * C)
        b2 = b_ref[rs, :]
        bl2 = b_ref[(c + 1) * C - 1:(c + 1) * C, :]
        kk = k_ref[rs, :]
        qd_ref[rs, :] = (q_ref[rs, :] * jnp.exp2(b2)).astype(BF16)
        ki_ref[rs, :] = (kk * jnp.exp2(-b2)).astype(BF16)
        kr_ref[rs, :] = (kk * jnp.exp2(bl2 - b2)).astype(BF16)
        dec_ref[c:c + 1, :] = jnp.exp2(bl2)

    def side_projection(i):
        p = i % n_slabs
        cols = slice(p * PW, (p + 1) * PW)
        if i < n_slabs:
            sg_ref[:, cols] = _silu(_dot(hb, w_in(3 * D + p * PW)))
        else:
            qx_ref[:, cols] = _dot(hb, w_in(4 * D + p * PW)).astype(BF16)

    def chunk_scores(c):
        rs = slice(c * C, (c + 1) * C)
        for hh in range(HG_HEADS):
            cols = slice(hh * hd, (hh + 1) * hd)
            a = _dot_nt(qd_ref[rs, cols], ki_ref[rs, cols])
            a_ref[c * HG_HEADS + hh] = jnp.where(causal, a, 0.0).astype(BF16)
            ut_ref[c * HG_HEADS + hh] = _dot_tn(v_ref[rs, cols], kr_ref[rs, cols])

    n_side = 2 * n_slabs
    per_chunk = -(-n_side // n_chunks)
    decay(0)
    for c in range(n_chunks):
        for i in range(c * per_chunk, min((c + 1) * per_chunk, n_side)):
            side_projection(i)
        if c + 1 < n_chunks:
            decay(c + 1)
        chunk_scores(c)

    probs = _attn_scores(qx_ref[...], kt_ref)

    def scan(c):
        for hh in range(HG_HEADS):
            cols = slice(hh * hd, (hh + 1) * hd)
            st = st_ref[hh]
            sp_ref[c * HG_HEADS + hh] = st.astype(BF16).T
            st_ref[hh] = st * dec_ref[c:c + 1, cols] + ut_ref[c * HG_HEADS + hh]

    def chunk_norm(c):
        rs = slice(c * C, (c + 1) * C)
        for hh in range(HG_HEADS):
            cols = slice(hh * hd, (hh + 1) * hd)
            o = oh_ref[rs, cols]
            o = o * lax.rsqrt(jnp.mean(o * o, axis=-1, keepdims=True) + EPS) * gn
            on_ref[rs, cols] = (o * sg_ref[rs, cols]).astype(BF16)

    mem_out = [None] * XA_HEADS

    def mem_values(h):
        def run():
            p, denom = probs[h]
            mem_out[h] = (_dot(p, mv_ref[0, 0, :, h * xa_dim:(h + 1) * xa_dim]) / denom).astype(BF16)
        return run

    def mem_project(h, n):
        def run():
            cs = slice(n * PW, (n + 1) * PW)
            part = _dot(mem_out[h], wout_ref[0, D + h * xa_dim:D + (h + 1) * xa_dim, n * PW:(n + 1) * PW].astype(BF16))
            if h == 0:
                o_ref[0, :, cs] = x_ref[0, :, cs] + part
            else:
                o_ref[0, :, cs] += part
        return run

    fillers = []
    for h in range(XA_HEADS):
        fillers.append(mem_values(h))
        fillers.extend(mem_project(h, n) for n in range(n_slabs))
    slots = n_chunks * (HG_HEADS // 2)
    per_slot = -(-len(fillers) // slots)

    def chunk_outputs(c):
        rs = slice(c * C, (c + 1) * C)
        for hh in range(HG_HEADS):
            cols = slice(hh * hd, (hh + 1) * hd)
            lhs = jnp.concatenate([qd_ref[rs, cols], a_ref[c * HG_HEADS + hh]], axis=1)
            rhs = jnp.concatenate([sp_ref[c * HG_HEADS + hh], v_ref[rs, cols]], axis=0)
            oh_ref[rs, cols] = _dot(lhs, rhs)
            if hh % 2 == 1:
                for _ in range(per_slot):
                    if fillers:
                        fillers.pop(0)()

    for c in range(n_chunks):
        scan(c)
        chunk_outputs(c)
        if c >= 1:
            chunk_norm(c - 1)
    while fillers:
        fillers.pop(0)()
    chunk_norm(n_chunks - 1)

    for p in range(n_slabs):
        o_ref[0] += _dot(on_ref[:, p * PW:(p + 1) * PW], w_out(p * PW, PW))


def _hgrn_mix(x, norm_stack, lb_logits, layer, slot, w_in_stack, gnorm_stack, kt, mv, w_out_stack):
    B, T, D = x.shape
    rows = HGRN_ROWS
    hd = D // HG_HEADS
    M = kt.shape[-1]
    xw = kt.shape[-2]
    assert T % rows == 0 and rows % HG_CHUNK == 0 and D % MXU_COLS == 0
    n_ch = rows // HG_CHUNK
    return pl.pallas_call(
        functools.partial(_hgrn_kernel, layer=layer, slot=slot),
        out_shape=jax.ShapeDtypeStruct((B, T, D), F32),
        grid=(B, T // rows),
        in_specs=[pl.BlockSpec((1, rows, D), lambda b, t: (b, t, 0)),
                  _const_spec(norm_stack.shape),
                  _const_spec(lb_logits.shape),
                  _layer_spec(w_in_stack.shape, slot),
                  _const_spec(gnorm_stack.shape),
                  pl.BlockSpec((1, 1, xw, M), lambda b, t: (layer, b, 0, 0)),
                  pl.BlockSpec((1, 1, M, xw), lambda b, t: (layer, b, 0, 0)),
                  _layer_spec(w_out_stack.shape, slot)],
        out_specs=pl.BlockSpec((1, rows, D), lambda b, t: (b, t, 0)),
        scratch_shapes=[pltpu.VMEM((rows, D), BF16),
                        pltpu.VMEM((rows, D), F32),
                        pltpu.VMEM((rows, D), F32),
                        pltpu.VMEM((rows, D), F32),
                        pltpu.VMEM((rows, D), BF16),
                        pltpu.VMEM((rows, D), F32),
                        pltpu.VMEM((rows, xw), BF16),
                        pltpu.VMEM((rows, D), BF16),
                        pltpu.VMEM((rows, D), BF16),
                        pltpu.VMEM((rows, D), BF16),
                        pltpu.VMEM((n_ch, D), F32),
                        pltpu.VMEM((n_ch * HG_HEADS, HG_CHUNK, HG_CHUNK), BF16),
                        pltpu.VMEM((n_ch * HG_HEADS, hd, hd), F32),
                        pltpu.VMEM((n_ch * HG_HEADS, hd, hd), BF16),
                        pltpu.VMEM((rows, D), F32),
                        pltpu.VMEM((rows, D), BF16),
                        pltpu.VMEM((HG_HEADS, hd, hd), F32)],
        compiler_params=_params(2),
        name="hgrn_mix",
    )(x, norm_stack, lb_logits, w_in_stack, gnorm_stack, kt, mv, w_out_stack)


def _gmlp_kernel(x_ref, g_ref, win_ref, lng_ref, lnb_ref, ws_ref, bs_ref, kt_ref, mv_ref, wout_ref, o_ref,
                 h_ref, v_ref, acc_ref, *, layer, slot):
    rows, D = x_ref.shape
    W = lng_ref.shape[1]
    C = GM_CHUNK
    gd = W // GM_GROUPS
    n_chunks = rows // C

    h_ref[...] = _rms(x_ref[...], g_ref[layer:layer + 1, :]).astype(BF16)
    hb = h_ref[...]

    def w_in(col0):
        return win_ref[0, :, col0:col0 + gd].astype(BF16)

    def w_out(row0, n_rows):
        return wout_ref[0, row0:row0 + n_rows, :].astype(BF16)

    gv = [_gelu(_dot(hb, w_in(W + g * gd))) for g in range(GM_GROUPS)]
    mu = functools.reduce(lambda a, b: a + b, [jnp.sum(t, axis=-1, keepdims=True) for t in gv]) / W

    xa_width = win_ref.shape[2] - 2 * W
    xa_dim = xa_width // XA_HEADS
    qx = jnp.concatenate([_dot(hb, w_in(2 * W + j * gd)).astype(BF16) for j in range(xa_width // gd)], axis=1)
    omem = _attn_values(_attn_scores(qx, kt_ref), mv_ref)
    acc_ref[...] = x_ref[...]
    for h in range(XA_HEADS):
        acc_ref[...] += _dot(omem[h], w_out(W + h * xa_dim, xa_dim))

    xc = [t - mu for t in gv]
    var = functools.reduce(lambda a, b: a + b, [jnp.sum(t * t, axis=-1, keepdims=True) for t in xc]) / W
    rstd = lax.rsqrt(var + EPS)
    for g in range(GM_GROUPS):
        cols = slice(g * gd, (g + 1) * gd)
        v_ref[:, cols] = (xc[g] * rstd * lng_ref[slot:slot + 1, cols] + lnb_ref[slot:slot + 1, cols]).astype(BF16)

    ri = lax.broadcasted_iota(jnp.int32, (C, C), 0)
    ci = lax.broadcasted_iota(jnp.int32, (C, C), 1)
    causal = ci <= ri

    def project_u(g):
        return _gelu(_dot(h_ref[...], w_in(g * gd)))

    def mix_and_project_out(g, u):
        cols = slice(g * gd, (g + 1) * gd)
        wm = jnp.where(causal, ws_ref[0, g], 0.0).astype(BF16)
        bias = bs_ref[0, :, g:g + 1]
        mixed = jnp.concatenate([_dot(wm, v_ref[c * C:(c + 1) * C, cols]) + bias for c in range(n_chunks)], axis=0)
        acc_ref[...] += _dot((u * mixed).astype(BF16), w_out(g * gd, gd))

    u_next = project_u(0)
    for g in range(GM_GROUPS):
        u = u_next
        if g + 1 < GM_GROUPS:
            u_next = project_u(g + 1)
        mix_and_project_out(g, u)

    o_ref[...] = acc_ref[...]


def _gmlp_mix(x2d, seq_len, norm_stack, layer, slot, w_in_stack, ln_g, ln_b, w_s, b_s_t, kt, mv, w_out_stack):
    N, D = x2d.shape
    rows = GMLP_ROWS
    M = kt.shape[-1]
    xw = kt.shape[-2]
    assert seq_len % rows == 0 and rows % GM_CHUNK == 0
    per_seq = seq_len // rows
    return pl.pallas_call(
        functools.partial(_gmlp_kernel, layer=layer, slot=slot),
        out_shape=jax.ShapeDtypeStruct((N, D), F32),
        grid=(N // rows,),
        in_specs=[pl.BlockSpec((rows, D), lambda i: (i, 0)),
                  _const_spec(norm_stack.shape),
                  _layer_spec(w_in_stack.shape, slot),
                  _const_spec(ln_g.shape),
                  _const_spec(ln_b.shape),
                  _layer_spec(w_s.shape, slot),
                  _layer_spec(b_s_t.shape, slot),
                  pl.BlockSpec((1, 1, xw, M), lambda i: (layer, i // per_seq, 0, 0)),
                  pl.BlockSpec((1, 1, M, xw), lambda i: (layer, i // per_seq, 0, 0)),
                  _layer_spec(w_out_stack.shape, slot)],
        out_specs=pl.BlockSpec((rows, D), lambda i: (i, 0)),
        scratch_shapes=[pltpu.VMEM((rows, D), BF16),
                        pltpu.VMEM((rows, ln_g.shape[1]), BF16),
                        pltpu.VMEM((rows, D), F32)],
        compiler_params=_params(1),
        name="gmlp_mix",
    )(x2d, norm_stack, w_in_stack, ln_g, ln_b, w_s, b_s_t, kt, mv, w_out_stack)


def kernel(x, mem, mem_norm, lb_logits, ffn1_norm, ffn1_w_in, ffn1_w_out, mix_norm, mem_w_kv,
           hgrn_w_in, hgrn_gnorm, hgrn_w_out, gmlp_w_in, gmlp_ln_g, gmlp_ln_b, gmlp_w_s, gmlp_b_s,
           gmlp_w_out, ffn2_norm, ffn2_w_in, ffn2_w_out, final_norm):
    B, T, D = x.shape
    depth = ffn1_norm.shape[0]

    kt, mv = _mem_kv(mem, mem_norm, mem_w_kv)
    gmlp_b_s_t = jnp.swapaxes(gmlp_b_s, 1, 2)
    x = x.reshape(B * T, D)
    for i in range(depth):
        x = _ffn(x, ffn1_norm, ffn1_w_in, ffn1_w_out, i)
        j = i // 2
        if i % 2 == 0:
            x = _hgrn_mix(x.reshape(B, T, D), mix_norm, lb_logits, i, j, hgrn_w_in, hgrn_gnorm, kt, mv,
                          hgrn_w_out).reshape(B * T, D)
        else:
            x = _gmlp_mix(x, T, mix_norm, i, j, gmlp_w_in, gmlp_ln_g, gmlp_ln_b, gmlp_w_s, gmlp_b_s_t, kt, mv,
                          gmlp_w_out)
        x = _ffn(x, ffn2_norm, ffn2_w_in, ffn2_w_out, i, final_norm if i == depth - 1 else None)
    return x.reshape(B, T, D)
```
